```python
import math
import jax, jax.numpy as jnp
from jax import lax
import numpy as np

D_MODEL = 2048
BATCH = 2
SEQ = 4096
DEPTH = 4
DEC_BATCH = 8
DEC_SEQ = 1
PAST_LEN = 16384
PAGE_SIZE = 128

N_META = 16
W_CONV = 1024
CONV_A_WIDTH = 31
W_SC = 1024
CONV_B_WIDTH = 3
N_HEADS_ATT = 8
DK_ATT = 64
DV_ATT = 2 * DK_ATT
N_BUCKETS = 32
MAX_DISTANCE = 128
Q_BLOCK = 128
N_HEADS_REC = 8
DK_REC = 128
DV_REC = 128
HGRN_CHUNK = 64
D_FF = 4 * D_MODEL
N_BRANCH = 4
EPS = 1e-6

ATT_QK = N_HEADS_ATT * 2 * DK_ATT
ATT_V = N_HEADS_ATT * DV_ATT
REC_K = N_HEADS_REC * DK_REC
REC_V = N_HEADS_REC * DV_REC
SEG_WIDTHS = (W_CONV, W_CONV, W_SC, W_SC, W_SC, ATT_QK, ATT_QK, ATT_V, REC_K, REC_K, REC_V, REC_V)
IN_WIDTH = sum(SEG_WIDTHS)

kernel_name = "hybrid_gated_parallel_decoder_step"


def rmsnorm(x, g):
    xf = x.astype(jnp.float32)
    y = xf * lax.rsqrt(jnp.mean(xf * xf, axis=-1, keepdims=True) + EPS)
    return (y * g.astype(jnp.float32)).astype(x.dtype)


def layernorm(x, g, b):
    xf = x.astype(jnp.float32)
    mu = jnp.mean(xf, axis=-1, keepdims=True)
    var = jnp.mean(jnp.square(xf - mu), axis=-1, keepdims=True)
    return ((xf - mu) * lax.rsqrt(var + EPS) * g.astype(jnp.float32) + b.astype(jnp.float32)).astype(x.dtype)


def split_proj(proj):
    idx = []
    acc = 0
    for w in SEG_WIDTHS[:-1]:
        acc += w
        idx.append(acc)
    return jnp.split(proj, idx, axis=-1)


def causal_dwconv(x, prefix, w):
    xin = jnp.concatenate([prefix.astype(x.dtype), x], axis=1)
    y = lax.conv_general_dilated(xin, w[:, None, :].astype(x.dtype), window_strides=(1,), padding='VALID',
                                 dimension_numbers=('NWC', 'WIO', 'NWC'), feature_group_count=x.shape[-1])
    return y, xin[:, xin.shape[1] - (w.shape[0] - 1):]


def rel_bias(qpos, kpos, table):
    n = jnp.maximum(qpos[:, None] - kpos[None, :], 0)
    max_exact = N_BUCKETS // 2
    nf = jnp.maximum(n, 1).astype(jnp.float32)
    large = max_exact + (jnp.log(nf / max_exact) / math.log(MAX_DISTANCE / max_exact)
                         * (N_BUCKETS - max_exact)).astype(jnp.int32)
    bucket = jnp.where(n < max_exact, n, jnp.minimum(large, N_BUCKETS - 1))
    return jnp.transpose(table[bucket], (2, 0, 1)).astype(jnp.float32)


def diff_attn_core(q, k, v, qpos, kpos, table, lam):
    s = jnp.einsum('bqhmd,bkhmd->bmhqk', q, k, preferred_element_type=jnp.float32) * (DK_ATT ** -0.5)
    s = s + rel_bias(qpos, kpos, table)[None, None]
    s = jnp.where(kpos[None, :] <= qpos[:, None], s, -jnp.inf)
    p = jax.nn.softmax(s, axis=-1)
    a = p[:, 0] - lam * p[:, 1]
    return jnp.einsum('bhqk,bkhd->bqhd', a.astype(v.dtype), v)


def diff_attn_prompt(q, k, v, table, lam):
    B, L = q.shape[0], q.shape[1]
    nb = -(-L // Q_BLOCK)
    Lp = nb * Q_BLOCK
    pad = ((0, 0), (0, Lp - L), (0, 0), (0, 0), (0, 0))
    qp = jnp.pad(q, pad)
    kp = jnp.pad(k, pad)
    vp = jnp.pad(v, pad[:4])
    kpos = jnp.arange(Lp)
    qb = jnp.moveaxis(qp.reshape(B, nb, Q_BLOCK, N_HEADS_ATT, 2, DK_ATT), 1, 0)

    def one_block(args):
        qblk, i = args
        qpos = i * Q_BLOCK + jnp.arange(Q_BLOCK)
        return diff_attn_core(qblk, kp, vp, qpos, kpos, table, lam)

    ob = lax.map(one_block, (qb, jnp.arange(nb)))
    return jnp.moveaxis(ob, 0, 1).reshape(B, Lp, N_HEADS_ATT, DV_ATT)[:, :L]


def hgrn2_chunked(q, k, v, logf, S0, chunk):
    B, L, H, DK = q.shape
    DV = v.shape[-1]
    n = L // chunk

    def blocks(t):
        return jnp.moveaxis(t.astype(jnp.float32).reshape(B, n, chunk, H, t.shape[-1]), 1, 0)

    causal = jnp.tril(jnp.ones((chunk, chunk), dtype=bool))

    def step(S, inp):
        qc, kc, vc, gc = inp
        b = jnp.cumsum(gc, axis=1)
        o_inter = jnp.einsum('bthk,bhkv->bthv', qc * jnp.exp(b), S)
        diff = b[:, :, None] - b[:, None, :]
        decay = jnp.exp(jnp.where(causal[None, :, :, None, None], diff, -jnp.inf))
        att = jnp.einsum('bthk,btshk,bshk->bhts', qc, decay, kc)
        o_intra = jnp.einsum('bhts,bshv->bthv', att, vc)
        b_last = b[:, -1]
        S_new = jnp.exp(b_last)[..., None] * S + jnp.einsum('bshk,bshv->bhkv', kc * jnp.exp(b_last[:, None] - b), vc)
        return S_new, o_inter + o_intra

    S_fin, o = lax.scan(step, S0.astype(jnp.float32), (blocks(q), blocks(k), blocks(v), blocks(logf)))
    return jnp.moveaxis(o, 0, 1).reshape(B, L, H, DV), S_fin


def layer_forward(h, pos, p, lam, lam_init, lb, rel_table, conv_a_prefix, conv_b_prefix, hgrn_S0,
                  past_k, past_v, hgrn_segments):
    B, L, _ = h.shape
    xn = rmsnorm(h, p['norm1'])
    a_u, a_g, b_b, b_c, b_h, c_q, c_k, c_v, d_q, d_f, d_i, d_g = split_proj(xn @ p['w_in'])

    glu = a_u * jax.nn.sigmoid(a_g)
    conv_a, new_conv_a = causal_dwconv(glu, conv_a_prefix, p['dw_a'])
    ya = jax.nn.silu(layernorm(conv_a + p['dw_a_bias'], p['ln_a_g'], p['ln_a_b'])) @ p['w_a_out']

    conv_b, new_conv_b = causal_dwconv(b_c * b_h, conv_b_prefix, p['conv_b'])
    yb = (b_b * conv_b) @ p['w_b_out']

    q = c_q.reshape(B, L, N_HEADS_ATT, 2, DK_ATT)
    k = c_k.reshape(B, L, N_HEADS_ATT, 2, DK_ATT)
    v = c_v.reshape(B, L, N_HEADS_ATT, DV_ATT)
    if past_k is None:
        o_c = diff_attn_prompt(q, k, v, rel_table, lam)
    else:
        k_all = jnp.concatenate([past_k.astype(k.dtype), k], axis=1)
        v_all = jnp.concatenate([past_v.astype(v.dtype), v], axis=1)
        kpos = jnp.arange(k_all.shape[1])
        o_c = diff_attn_core(q, k_all, v_all, pos, kpos, rel_table, lam)
    o_c = rmsnorm(o_c, p['subln']) * (1.0 - lam_init)
    yc = o_c.reshape(B, L, ATT_V).astype(h.dtype) @ p['w_c_out']

    zf = d_f.astype(jnp.float32).reshape(B, L, N_HEADS_REC, DK_REC)
    logf = jnp.logaddexp(jnp.log(lb), jnp.log1p(-lb) + jax.nn.log_sigmoid(zf))
    kd = (1.0 - lb) * jax.nn.sigmoid(-zf)
    qd = jax.nn.silu(d_q).reshape(B, L, N_HEADS_REC, DK_REC)
    vd = d_i.reshape(B, L, N_HEADS_REC, DV_REC)
    S = hgrn_S0
    outs = []
    for (s0, s1, c) in hgrn_segments:
        o_seg, S = hgrn2_chunked(qd[:, s0:s1], kd[:, s0:s1], vd[:, s0:s1], logf[:, s0:s1], S, c)
        outs.append(o_seg)
    o_d = jnp.concatenate(outs, axis=1)
    yd = (rmsnorm(o_d, p['g_norm_d']).reshape(B, L, REC_V) * jax.nn.silu(d_g)).astype(h.dtype) @ p['w_d_out']

    gates = jax.nn.sigmoid(xn @ p['w_gate'] + p['b_gate']).reshape(B, L, N_BRANCH, D_MODEL)
    merged = gates[:, :, 0] * ya + gates[:, :, 1] * yb + gates[:, :, 2] * yc + gates[:, :, 3] * yd
    h = h + merged @ p['w_o']

    hn = rmsnorm(h, p['norm2'])
    h = h + jnp.square(jax.nn.relu(hn @ p['w_up'])) @ p['w_down']
    return h, k.reshape(B, L, N_HEADS_ATT, 2 * DK_ATT), v, new_conv_a, new_conv_b, S


def setup_inputs(seed: int = 0) -> dict:
    key = jax.random.key(seed)
    ks = iter(jax.random.split(key, 40))
    f32 = jnp.float32
    n_pages = PAST_LEN // PAGE_SIZE
    n_pool = (DEC_BATCH * n_pages * 5) // 4

    def nrm(shape, scale):
        return jax.random.normal(next(ks), shape, f32) * scale

    def gain(shape):
        return 1.0 + nrm(shape, 0.01)

    perm = jax.random.permutation(next(ks), n_pool)[:DEC_BATCH * n_pages]
    page_table = perm.reshape(DEC_BATCH, n_pages).astype(jnp.int32)
    return {
        "x_prompt": nrm((BATCH, SEQ, D_MODEL), 1.0),
        "x_sample": nrm((DEC_BATCH, DEC_SEQ, D_MODEL), 1.0),
        "cache_k": nrm((n_pool, DEPTH, PAGE_SIZE, N_HEADS_ATT, 2 * DK_ATT), 1.0),
        "cache_v": nrm((n_pool, DEPTH, PAGE_SIZE, N_HEADS_ATT, DV_ATT), 1.0),
        "state_conv_a": nrm((DEPTH, DEC_BATCH, CONV_A_WIDTH - 1, W_CONV), 0.5),
        "state_conv_b": nrm((DEPTH, DEC_BATCH, CONV_B_WIDTH - 1, W_SC), 0.5),
        "state_hgrn": nrm((DEPTH, DEC_BATCH, N_HEADS_REC, DK_REC, DV_REC), 0.5),
        "page_table": page_table,
        "meta_tokens": nrm((N_META, D_MODEL), 1.0),
        "rel_bias_table": nrm((N_BUCKETS, N_HEADS_ATT), 0.5),
        "hgrn_lower_bound": nrm((DEPTH, REC_K), 0.5),
        "norm1": gain((DEPTH, D_MODEL)),
        "w_in": nrm((DEPTH, D_MODEL, IN_WIDTH), D_MODEL ** -0.5),
        "dw_a": nrm((DEPTH, CONV_A_WIDTH, W_CONV), CONV_A_WIDTH ** -0.5),
        "dw_a_bias": nrm((DEPTH, W_CONV), 0.01),
        "ln_a_g": gain((DEPTH, W_CONV)),
        "ln_a_b": nrm((DEPTH, W_CONV), 0.01),
        "w_a_out": nrm((DEPTH, W_CONV, D_MODEL), W_CONV ** -0.5),
        "conv_b": nrm((DEPTH, CONV_B_WIDTH, W_SC), CONV_B_WIDTH ** -0.5),
        "w_b_out": nrm((DEPTH, W_SC, D_MODEL), W_SC ** -0.5),
        "lam_q1": nrm((DEPTH, DK_ATT), 0.1),
        "lam_k1": nrm((DEPTH, DK_ATT), 0.1),
        "lam_q2": nrm((DEPTH, DK_ATT), 0.1),
        "lam_k2": nrm((DEPTH, DK_ATT), 0.1),
        "subln": gain((DEPTH, DV_ATT)),
        "w_c_out": nrm((DEPTH, ATT_V, D_MODEL), ATT_V ** -0.5),
        "g_norm_d": gain((DEPTH, DV_REC)),
        "w_d_out": nrm((DEPTH, REC_V, D_MODEL), REC_V ** -0.5),
        "w_gate": nrm((DEPTH, D_MODEL, N_BRANCH * D_MODEL), D_MODEL ** -0.5),
        "b_gate": nrm((DEPTH, N_BRANCH * D_MODEL), 0.01),
        "w_o": nrm((DEPTH, D_MODEL, D_MODEL), D_MODEL ** -0.5),
        "norm2": gain((DEPTH, D_MODEL)),
        "w_up": nrm((DEPTH, D_MODEL, D_FF), D_MODEL ** -0.5),
        "w_down": nrm((DEPTH, D_FF, D_MODEL), D_FF ** -0.5),
        "final_norm": gain((D_MODEL,)),
    }


def reference(x_prompt, x_sample, cache_k, cache_v, state_conv_a, state_conv_b, state_hgrn, page_table,
              meta_tokens, rel_bias_table, hgrn_lower_bound, norm1, w_in, dw_a, dw_a_bias, ln_a_g, ln_a_b,
              w_a_out, conv_b, w_b_out, lam_q1, lam_k1, lam_q2, lam_k2, subln, w_c_out, g_norm_d, w_d_out,
              w_gate, b_gate, w_o, norm2, w_up, w_down, final_norm):
    f32 = jnp.float32
    B = x_prompt.shape[0]
    DB, DS = x_sample.shape[0], x_sample.shape[1]
    past_len = page_table.shape[1] * cache_k.shape[2]

    hp = jnp.concatenate([jnp.broadcast_to(meta_tokens[None].astype(x_prompt.dtype), (B, N_META, D_MODEL)),
                          x_prompt], axis=1)
    Lt = hp.shape[1]
    pos_p = jnp.arange(Lt)
    hs = x_sample
    pos_s = past_len + jnp.arange(DS)

    prompt_segments = ((0, N_META, N_META), (N_META, Lt, HGRN_CHUNK))
    sample_chunk = HGRN_CHUNK if DS % HGRN_CHUNK == 0 else DS
    sample_segments = ((0, DS, sample_chunk),)

    lbs = jnp.cumsum(jax.nn.softmax(hgrn_lower_bound.astype(f32), axis=0), axis=0)
    lbs = lbs - lbs[0:1]

    zeros_a = jnp.zeros((B, CONV_A_WIDTH - 1, W_CONV), x_prompt.dtype)
    zeros_b = jnp.zeros((B, CONV_B_WIDTH - 1, W_SC), x_prompt.dtype)
    zeros_S = jnp.zeros((B, N_HEADS_REC, DK_REC, DV_REC), f32)

    kp_l, vp_l, cap_l, cbp_l, sp_l = [], [], [], [], []
    ks_l, vs_l, cas_l, cbs_l, ss_l = [], [], [], [], []
    for l in range(DEPTH):
        p = {
            'norm1': norm1[l], 'w_in': w_in[l], 'dw_a': dw_a[l], 'dw_a_bias': dw_a_bias[l],
            'ln_a_g': ln_a_g[l], 'ln_a_b': ln_a_b[l], 'w_a_out': w_a_out[l], 'conv_b': conv_b[l],
            'w_b_out': w_b_out[l], 'subln': subln[l], 'w_c_out': w_c_out[l], 'g_norm_d': g_norm_d[l],
            'w_d_out': w_d_out[l], 'w_gate': w_gate[l], 'b_gate': b_gate[l], 'w_o': w_o[l],
            'norm2': norm2[l], 'w_up': w_up[l], 'w_down': w_down[l],
        }
        lam_init = 0.8 - 0.6 * math.exp(-0.3 * l)
        lam = (jnp.exp(jnp.sum(lam_q1[l].astype(f32) * lam_k1[l].astype(f32)))
               - jnp.exp(jnp.sum(lam_q2[l].astype(f32) * lam_k2[l].astype(f32))) + lam_init)
        lb = lbs[l].reshape(N_HEADS_REC, DK_REC)

        hp, kn, vn, ca, cb, S = layer_forward(hp, pos_p, p, lam, lam_init, lb, rel_bias_table,
                                              zeros_a, zeros_b, zeros_S, None, None, prompt_segments)
        kp_l.append(kn); vp_l.append(vn); cap_l.append(ca); cbp_l.append(cb); sp_l.append(S)

        past_k = cache_k[page_table, l].reshape(DB, past_len, N_HEADS_ATT, 2, DK_ATT)
        past_v = cache_v[page_table, l].reshape(DB, past_len, N_HEADS_ATT, DV_ATT)
        hs, kn, vn, ca, cb, S = layer_forward(hs, pos_s, p, lam, lam_init, lb, rel_bias_table,
                                              state_conv_a[l], state_conv_b[l], state_hgrn[l],
                                              past_k, past_v, sample_segments)
        ks_l.append(kn); vs_l.append(vn); cas_l.append(ca); cbs_l.append(cb); ss_l.append(S)

    y_prompt = rmsnorm(hp, final_norm)[:, N_META:]
    y_sample = rmsnorm(hs, final_norm)
    new_k_prompt = jnp.stack(kp_l, axis=2)
    new_v_prompt = jnp.stack(vp_l, axis=2)
    new_conv_a_prompt = jnp.stack(cap_l, axis=0)
    new_conv_b_prompt = jnp.stack(cbp_l, axis=0)
    new_hgrn_prompt = jnp.stack(sp_l, axis=0)
    new_k_sample = jnp.stack(ks_l, axis=2)
    new_v_sample = jnp.stack(vs_l, axis=2)
    new_conv_a_sample = jnp.stack(cas_l, axis=0)
    new_conv_b_sample = jnp.stack(cbs_l, axis=0)
    new_hgrn_sample = jnp.stack(ss_l, axis=0)
    return (y_prompt, y_sample, new_k_prompt, new_v_prompt, new_conv_a_prompt, new_conv_b_prompt, new_hgrn_prompt,
            new_k_sample, new_v_sample, new_conv_a_sample, new_conv_b_sample, new_hgrn_sample)
```

```python
import functools
import math

import jax
import jax.numpy as jnp
from jax import lax
from jax.experimental import pallas as pl
from jax.experimental.pallas import tpu as pltpu

F32 = jnp.float32
BF16 = jnp.bfloat16

EPS = 1e-6
MAX_DISTANCE = 128
NEG = -1e30

LANES = 128
SEQ_TILE = 384
ROW_TILE = 768
COL_TILE = 512
FF_TILE = 512
SUB = 16
PAGES_PER_STEP = 8
VMEM_LIMIT = 56 * 1024 * 1024
N_SEG = 12
N_BRANCH = 4


def _cparams(sem):
    return pltpu.CompilerParams(dimension_semantics=sem, vmem_limit_bytes=VMEM_LIMIT)


def _sigmoid(x):
    return 1.0 / (1.0 + jnp.exp(-x))


def _rms(x, g):
    return x * lax.rsqrt(jnp.mean(x * x, axis=-1, keepdims=True) + EPS) * g


def _mm_kernel(*refs, has_norm, has_bias, act, has_res):
    it = iter(refs)
    x_ref = next(it)
    g_ref = next(it) if has_norm else None
    w_ref = next(it)
    b_ref = next(it) if has_bias else None
    r_ref = next(it) if has_res else None
    o_ref = next(it)
    if has_norm:
        xn_ref = next(it)

        @pl.when(pl.program_id(1) == 0)
        def _():
            xn_ref[...] = _rms(x_ref[...], g_ref[...]).astype(BF16)

        lhs = xn_ref[...]
    else:
        lhs = x_ref[...]
    y = jnp.dot(lhs, w_ref[...], preferred_element_type=F32)
    if has_bias:
        y = y + b_ref[...]
    if act == "sigmoid":
        y = _sigmoid(y)
    if has_res:
        y = y + r_ref[...]
    o_ref[...] = y.astype(o_ref.dtype)


def _matmul(x, w, *, tm, tn, out_dtype, norm_g=None, bias=None, act=None, residual=None, name):
    M, K = x.shape
    N = w.shape[1]
    tn = min(tn, N)
    assert M % tm == 0 and N % tn == 0
    args = [x]
    specs = [pl.BlockSpec((tm, K), lambda i, j: (i, 0))]
    scratch = []
    if norm_g is not None:
        args.append(norm_g.reshape(1, K))
        specs.append(pl.BlockSpec((1, K), lambda i, j: (0, 0)))
        scratch.append(pltpu.VMEM((tm, K), BF16))
    args.append(w)
    specs.append(pl.BlockSpec((K, tn), lambda i, j: (0, j)))
    if bias is not None:
        args.append(bias.reshape(1, N))
        specs.append(pl.BlockSpec((1, tn), lambda i, j: (0, j)))
    if residual is not None:
        args.append(residual)
        specs.append(pl.BlockSpec((tm, tn), lambda i, j: (i, j)))
    kern = functools.partial(_mm_kernel, has_norm=norm_g is not None, has_bias=bias is not None,
                             act=act, has_res=residual is not None)
    return pl.pallas_call(
        kern,
        out_shape=jax.ShapeDtypeStruct((M, N), out_dtype),
        grid=(M // tm, N // tn),
        in_specs=specs,
        out_specs=pl.BlockSpec((tm, tn), lambda i, j: (i, j)),
        scratch_shapes=scratch,
        compiler_params=_cparams(("parallel", "arbitrary")),
        name=name,
    )(*args)


def _merge_kernel(pa_ref, pb_ref, pc_ref, pd_ref, w_ref, g0_ref, g1_ref, g2_ref, g3_ref, o_ref):
    acc = None
    for k, (p_ref, g_ref) in enumerate(((pa_ref, g0_ref), (pb_ref, g1_ref), (pc_ref, g2_ref), (pd_ref, g3_ref))):
        y = jnp.dot(p_ref[...], w_ref[k], preferred_element_type=F32) * g_ref[...].astype(F32)
        acc = y if acc is None else acc + y
    o_ref[...] = acc.astype(o_ref.dtype)


def _merge(pres, w_branch, gates, *, tm, tn):
    R, W = pres[0].shape
    D = w_branch.shape[2]
    tn = min(tn, D)
    n_col = D // tn
    pre_spec = pl.BlockSpec((tm, W), lambda i, j: (i, 0))
    gate_specs = [pl.BlockSpec((tm, tn), functools.partial(lambda i, j, b: (i, b * n_col + j), b=b))
                  for b in range(N_BRANCH)]
    return pl.pallas_call(
        _merge_kernel,
        out_shape=jax.ShapeDtypeStruct((R, D), BF16),
        grid=(R // tm, n_col),
        in_specs=[pre_spec] * N_BRANCH + [pl.BlockSpec((N_BRANCH, W, tn), lambda i, j: (0, 0, j))] + gate_specs,
        out_specs=pl.BlockSpec((tm, tn), lambda i, j: (i, j)),
        compiler_params=_cparams(("parallel", "arbitrary")),
        name="merge",
    )(*pres, w_branch, gates, gates, gates, gates)


def _mlp_kernel(*refs, final):
    if final:
        h_ref, g_ref, wu_ref, wd_ref, fn_ref, o_ref, hn_ref, acc_ref = refs
    else:
        h_ref, g_ref, wu_ref, wd_ref, o_ref, hn_ref, acc_ref = refs
    f = pl.program_id(1)

    @pl.when(f == 0)
    def _():
        hn_ref[...] = _rms(h_ref[...], g_ref[...]).astype(BF16)
        acc_ref[...] = h_ref[...]

    u = jnp.dot(hn_ref[...], wu_ref[...], preferred_element_type=F32)
    u = jnp.square(jnp.maximum(u, 0.0))
    acc_ref[...] += jnp.dot(u.astype(BF16), wd_ref[...], preferred_element_type=F32)

    @pl.when(f == pl.num_programs(1) - 1)
    def _():
        y = acc_ref[...]
        if final:
            y = _rms(y, fn_ref[...])
        o_ref[...] = y


def _mlp(h, norm_g, w_up, w_down, final_g, *, tm, tf):
    R, D = h.shape
    FF = w_up.shape[1]
    final = final_g is not None
    args = [h, norm_g.reshape(1, D), w_up, w_down]
    specs = [pl.BlockSpec((tm, D), lambda i, f: (i, 0)),
             pl.BlockSpec((1, D), lambda i, f: (0, 0)),
             pl.BlockSpec((D, tf), lambda i, f: (0, f)),
             pl.BlockSpec((tf, D), lambda i, f: (f, 0))]
    if final:
        args.append(final_g.reshape(1, D))
        specs.append(pl.BlockSpec((1, D), lambda i, f: (0, 0)))
    return pl.pallas_call(
        functools.partial(_mlp_kernel, final=final),
        out_shape=jax.ShapeDtypeStruct((R, D), F32),
        grid=(R // tm, FF // tf),
        in_specs=specs,
        out_specs=pl.BlockSpec((tm, D), lambda i, f: (i, 0)),
        scratch_shapes=[pltpu.VMEM((tm, D), BF16), pltpu.VMEM((tm, D), F32)],
        compiler_params=_cparams(("parallel", "arbitrary")),
        name="mlp",
    )(*args)


def _conv_kernel(au_ref, ag_ref, bb_ref, bc_ref, bh_ref, dwa_ref, dwab_ref, lng_ref, lnb_ref, cb_ref,
                 pa_ref, pb_ref, sa_ref, sb_ref, xa_ref, xb_ref, *, T, n_pad, wa, wb):
    t = pl.program_id(1)
    ha = 32
    hb = 8
    glu = au_ref[...] * _sigmoid(ag_ref[...])
    u = bc_ref[...] * bh_ref[...]

    @pl.when(t == 0)
    def _():
        valid = lax.broadcasted_iota(jnp.int32, (T, 1), 0) >= n_pad
        xa_ref[0:ha, :] = jnp.zeros((ha, xa_ref.shape[1]), F32)
        xb_ref[0:hb, :] = jnp.zeros((hb, xb_ref.shape[1]), F32)
        xa_ref[ha:ha + T, :] = jnp.where(valid, glu, 0.0)
        xb_ref[hb:hb + T, :] = jnp.where(valid, u, 0.0)

    @pl.when(t > 0)
    def _():
        xa_ref[0:ha, :] = xa_ref[T:T + ha, :]
        xb_ref[0:hb, :] = xb_ref[T:T + hb, :]
        xa_ref[ha:ha + T, :] = glu
        xb_ref[hb:hb + T, :] = u

    def chunk(c, carry):
        r0 = pl.multiple_of(c * SUB, SUB)
        cols = []
        for g in range(xa_ref.shape[1] // LANES):
            ls = slice(g * LANES, (g + 1) * LANES)
            win = xa_ref[pl.ds(r0, ha + SUB), ls]
            a = None
            for j in range(wa):
                o = ha - (wa - 1) + j
                term = dwa_ref[j:j + 1, ls] * win[o:o + SUB, :]
                a = term if a is None else a + term
            cols.append(a)
        acc = jnp.concatenate(cols, axis=1) + dwab_ref[...]
        mu = jnp.mean(acc, axis=-1, keepdims=True)
        cen = acc - mu
        var = jnp.mean(cen * cen, axis=-1, keepdims=True)
        y = cen * lax.rsqrt(var + EPS) * lng_ref[...] + lnb_ref[...]
        pa_ref[pl.ds(r0, SUB), :] = (y * _sigmoid(y)).astype(pa_ref.dtype)
        winb = xb_ref[pl.ds(r0, hb + SUB), :]
        cv = None
        for j in range(wb):
            o = hb - (wb - 1) + j
            term = cb_ref[j:j + 1, :] * winb[o:o + SUB, :]
            cv = term if cv is None else cv + term
        pb_ref[pl.ds(r0, SUB), :] = (bb_ref[pl.ds(r0, SUB), :] * cv).astype(pb_ref.dtype)
        return carry

    lax.fori_loop(0, T // SUB, chunk, 0)

    @pl.when(t == pl.num_programs(1) - 1)
    def _():
        sa_ref[0] = xa_ref[T + ha - (wa - 1):T + ha, :]
        sb_ref[0] = xb_ref[T + hb - (wb - 1):T + hb, :]


def _conv(proj, dw_a, dw_a_bias, ln_g, ln_b, conv_b, *, B, LB, n_pad, W):
    T = SEQ_TILE
    nT = LB // T
    wa, wb = dw_a.shape[0], conv_b.shape[0]
    assert wa - 1 <= 32 and wb - 1 <= 8 and n_pad <= T
    R = proj.shape[0]

    def seg(c):
        return pl.BlockSpec((T, W), lambda b, t: (b * nT + t, c))

    def par(n):
        return pl.BlockSpec((n, W), lambda b, t: (0, 0))

    row = pl.BlockSpec((T, W), lambda b, t: (b * nT + t, 0))
    return pl.pallas_call(
        functools.partial(_conv_kernel, T=T, n_pad=n_pad, wa=wa, wb=wb),
        out_shape=(jax.ShapeDtypeStruct((R, W), BF16), jax.ShapeDtypeStruct((R, W), BF16),
                   jax.ShapeDtypeStruct((B, wa - 1, W), F32), jax.ShapeDtypeStruct((B, wb - 1, W), F32)),
        grid=(B, nT),
        in_specs=[seg(0), seg(1), seg(2), seg(3), seg(4), par(wa), par(1), par(1), par(1), par(wb)],
        out_specs=(row, row,
                   pl.BlockSpec((1, wa - 1, W), lambda b, t: (b, 0, 0)),
                   pl.BlockSpec((1, wb - 1, W), lambda b, t: (b, 0, 0))),
        scratch_shapes=[pltpu.VMEM((T + 32, W), F32), pltpu.VMEM((T + 8, W), F32)],
        compiler_params=_cparams(("parallel", "arbitrary")),
        name="conv",
    )(proj, proj, proj, proj, proj, dw_a, dw_a_bias.reshape(1, W), ln_g.reshape(1, W), ln_b.reshape(1, W), conv_b)


B_DIAG, B_SUB, B_00, B_10, B_X0 = 1, 2, 3, 4, 5


def _rel_bias(qpos, kpos, table, n_buckets):
    n = jnp.maximum(qpos[:, None] - kpos[None, :], 0)
    max_exact = n_buckets // 2
    nf = jnp.maximum(n, 1).astype(F32)
    large = max_exact + (jnp.log(nf / max_exact) / math.log(MAX_DISTANCE / max_exact)
                         * (n_buckets - max_exact)).astype(jnp.int32)
    bucket = jnp.where(n < max_exact, n, jnp.minimum(large, n_buckets - 1))
    bias = jnp.transpose(table[bucket], (2, 0, 1)).astype(F32)
    return bias - table[n_buckets - 1].astype(F32)[:, None, None]


def _bias_bank(table, T, n_pad):
    nb, H = table.shape
    r = jnp.arange(T)
    causal = (r[None, :] <= r[:, None])[None]
    kpad = (r < n_pad)[None, None, :]
    qval = (r >= n_pad)[None, :, None]
    zero = jnp.zeros((H, T, T), F32)
    diag = jnp.where(causal, _rel_bias(r, r, table, nb), NEG)
    sub = _rel_bias(r + T, r, table, nb)
    b00 = jnp.where(kpad & qval, NEG, diag)
    b10 = jnp.where(kpad, NEG, sub)
    bx0 = jnp.where(kpad, NEG, zero)
    return jnp.stack([zero, diag, sub, b00, b10, bx0], axis=0)


def _attn_kernel(lam_ref, q_ref, k_ref, v_ref, bias_ref, sub_ref, o_ref, m_ref, l_ref, acc_ref,
                 *, T, dk, scale, out_scale):
    i = pl.program_id(2)
    q = q_ref[...] * scale
    lane = lax.broadcasted_iota(jnp.int32, q.shape, 1)
    qm = (jnp.where(lane < dk, q, 0.0).astype(BF16), jnp.where(lane >= dk, q, 0.0).astype(BF16))

    m_ref[...] = jnp.full(m_ref.shape, NEG, F32)
    l_ref[...] = jnp.zeros(l_ref.shape, F32)
    acc_ref[...] = jnp.zeros(acc_ref.shape, F32)

    def tile(j, bias):
        r0 = pl.multiple_of(j * T, T)
        k = k_ref[pl.ds(r0, T), :].astype(BF16)
        v = v_ref[pl.ds(r0, T), :].astype(BF16)
        for mp in range(2):
            s = lax.dot_general(qm[mp], k, (((1,), (1,)), ((), ())), preferred_element_type=F32)
            if bias is not None:
                s = s + bias
            m_old = m_ref[mp]
            m_new = jnp.maximum(m_old, jnp.max(s, axis=-1, keepdims=True))
            alpha = jnp.exp(m_old - m_new)
            p = jnp.exp(s - m_new)
            l_ref[mp] = alpha * l_ref[mp] + jnp.sum(p, axis=-1, keepdims=True)
            acc_ref[mp] = alpha * acc_ref[mp] + jnp.dot(p.astype(BF16), v, preferred_element_type=F32)
            m_ref[mp] = m_new

    idx0 = jnp.where(i == 0, B_00, jnp.where(i == 1, B_10, B_X0))
    tile(0, bias_ref[idx0, 0])

    def far(j, carry):
        tile(j, None)
        return carry

    lax.fori_loop(1, i - 1, far, 0)

    @pl.when(i >= 2)
    def _():
        tile(i - 1, bias_ref[B_SUB, 0])

    @pl.when(i >= 1)
    def _():
        tile(i, bias_ref[B_DIAG, 0])

    lam = lam_ref[0]
    o = acc_ref[0] / l_ref[0] - lam * (acc_ref[1] / l_ref[1])
    o_ref[...] = (_rms(o, sub_ref[...]) * out_scale).astype(o_ref.dtype)


def _attn(proj, bank, lam, subln, *, B, LB, H, dv, lam_init):
    T = SEQ_TILE
    nQ = LB // T
    R = proj.shape[0]
    dk = dv // 2
    qc, kc, vc = 5 * H, 6 * H, 7 * H
    return pl.pallas_call(
        functools.partial(_attn_kernel, T=T, dk=dk, scale=dk ** -0.5, out_scale=1.0 - lam_init),
        out_shape=jax.ShapeDtypeStruct((R, H * dv), BF16),
        grid=(B, H, nQ),
        in_specs=[pl.BlockSpec(memory_space=pltpu.SMEM),
                  pl.BlockSpec((T, dv), lambda b, h, i: (b * nQ + i, qc + h)),
                  pl.BlockSpec((LB, dv), lambda b, h, i: (b, kc + h)),
                  pl.BlockSpec((LB, dv), lambda b, h, i: (b, vc + h)),
                  pl.BlockSpec((6, 1, T, T), lambda b, h, i: (0, h, 0, 0)),
                  pl.BlockSpec((1, dv), lambda b, h, i: (0, 0))],
        out_specs=pl.BlockSpec((T, dv), lambda b, h, i: (b * nQ + i, h)),
        scratch_shapes=[pltpu.VMEM((2, T, 1), F32), pltpu.VMEM((2, T, 1), F32), pltpu.VMEM((2, T, dv), F32)],
        compiler_params=_cparams(("parallel", "parallel", "arbitrary")),
        name="attn",
    )(lam, proj, proj, proj, bank, subln.reshape(1, dv))


def _log_forget(zf, log_lb, log1m_lb):
    ls = jnp.minimum(zf, 0.0) - jnp.log1p(jnp.exp(-jnp.abs(zf)))
    b = log1m_lb + ls
    hi = jnp.maximum(log_lb, b)
    lo = jnp.minimum(log_lb, b)
    return hi + jnp.log1p(jnp.exp(lo - hi))


def _split3(x):
    hi = x.astype(BF16)
    r = x - hi.astype(F32)
    mid = r.astype(BF16)
    lo = (r - mid.astype(F32)).astype(BF16)
    return hi, mid, lo


def _hgrn_kernel(q_ref, f_ref, i_ref, g_ref, loglb_ref, log1m_ref, omlb_ref, gn_ref,
                 o_ref, s_ref, st_ref, *, T, n_pad, H, dk):
    t = pl.program_id(1)

    @pl.when(t == 0)
    def _():
        st_ref[...] = jnp.zeros(st_ref.shape, F32)

    rr = lax.broadcasted_iota(jnp.int32, (SUB, SUB), 0)
    cc = lax.broadcasted_iota(jnp.int32, (SUB, SUB), 1)
    tri = jnp.where(cc <= rr, 1.0, 0.0).astype(BF16)
    row = lax.broadcasted_iota(jnp.int32, (SUB, 1), 0)

    def chunk(c, carry):
        r0 = pl.multiple_of(c * SUB, SUB)
        zf = f_ref[pl.ds(r0, SUB), :]
        logf = _log_forget(zf, loglb_ref[...], log1m_ref[...])
        hi, mid, lo = _split3(logf)
        bcum = (jnp.dot(tri, hi, preferred_element_type=F32) + jnp.dot(tri, mid, preferred_element_type=F32)
                + jnp.dot(tri, lo, preferred_element_type=F32))
        kd = omlb_ref[...] * _sigmoid(-zf)
        zq = q_ref[pl.ds(r0, SUB), :]
        qd = zq * _sigmoid(zq)
        seq_row = t * T + r0 + row
        vd = jnp.where(seq_row >= n_pad, i_ref[pl.ds(r0, SUB), :], 0.0)
        zg = g_ref[pl.ds(r0, SUB), :]
        og = zg * _sigmoid(zg)
        for h in range(H):
            sl = slice(h * dk, (h + 1) * dk)
            b, qh, kh, vh = bcum[:, sl], qd[:, sl], kd[:, sl], vd[:, sl]
            blast = b[SUB - 1:SUB, :]
            st = st_ref[h]
            o = lax.dot_general((qh * jnp.exp(b)).astype(BF16), st.astype(BF16),
                                (((1,), (1,)), ((), ())), preferred_element_type=F32)
            for d in range(SUB):
                if d == 0:
                    kr, br, vr = kh, b, vh
                else:
                    kr, br, vr = pltpu.roll(kh, d, 0), pltpu.roll(b, d, 0), pltpu.roll(vh, d, 0)
                w = jnp.where(row >= d, qh * kr * jnp.exp(b - br), 0.0)
                o = o + jnp.sum(w, axis=-1, keepdims=True) * vr
            kt = (kh * jnp.exp(blast - b)).astype(BF16)
            kv = lax.dot_general(vh.astype(BF16), kt, (((0,), (0,)), ((), ())), preferred_element_type=F32)
            st_ref[h] = st * jnp.exp(blast) + kv
            o_ref[pl.ds(r0, SUB), sl] = (_rms(o, gn_ref[...]) * og[:, sl]).astype(o_ref.dtype)
        return carry

    lax.fori_loop(0, T // SUB, chunk, 0)

    @pl.when(t == pl.num_programs(1) - 1)
    def _():
        for h in range(H):
            s_ref[0, h] = st_ref[h].T


def _hgrn(proj, log_lb, log1m_lb, om_lb, g_norm, *, B, LB, n_pad, H, dk):
    T = SEQ_TILE
    nT = LB // T
    R = proj.shape[0]
    W = H * dk

    def seg(c):
        return pl.BlockSpec((T, W), lambda b, t: (b * nT + t, c))

    def par(n):
        return pl.BlockSpec((1, n), lambda b, t: (0, 0))

    return pl.pallas_call(
        functools.partial(_hgrn_kernel, T=T, n_pad=n_pad, H=H, dk=dk),
        out_shape=(jax.ShapeDtypeStruct((R, W), BF16), jax.ShapeDtypeStruct((B, H, dk, dk), F32)),
        grid=(B, nT),
        in_specs=[seg(8), seg(9), seg(10), seg(11), par(W), par(W), par(W), par(dk)],
        out_specs=(pl.BlockSpec((T, W), lambda b, t: (b * nT + t, 0)),
                   pl.BlockSpec((1, H, dk, dk), lambda b, t: (b, 0, 0, 0))),
        scratch_shapes=[pltpu.VMEM((H, dk, dk), F32)],
        compiler_params=_cparams(("parallel", "arbitrary")),
        name="hgrn",
    )(proj, proj, proj, proj, log_lb, log1m_lb, om_lb, g_norm.reshape(1, dk))


def _sample_mix_kernel(p_ref, sa_ref, sb_ref, sh_ref, dwa_ref, dwab_ref, lng_ref, lnb_ref, cb_ref,
                       lb_ref, omlb_ref, gn_ref, pa_in, pb_in, pd_in,
                       pa_ref, pb_ref, pd_ref, nsa_ref, nsb_ref, nsh_ref, o_scr, *, NB, W, H, dk, wa, wb):
    del pa_in, pb_in, pd_in

    def seg(c):
        return p_ref[:, c * W:(c + 1) * W]

    pad = jnp.zeros((SUB - NB, W), F32)

    glu = seg(0) * _sigmoid(seg(1))
    rows = []
    for b in range(NB):
        cv = jnp.sum(sa_ref[b] * dwa_ref[0:wa - 1, :], axis=0, keepdims=True)
        rows.append(cv + dwa_ref[wa - 1:wa, :] * glu[b:b + 1, :])
        nsa_ref[b, 0:wa - 2, :] = sa_ref[b, 1:wa - 1, :]
        nsa_ref[b, wa - 2:wa - 1, :] = glu[b:b + 1, :]
    acc = jnp.concatenate(rows, axis=0) + dwab_ref[...]
    mu = jnp.mean(acc, axis=-1, keepdims=True)
    cen = acc - mu
    var = jnp.mean(cen * cen, axis=-1, keepdims=True)
    y = cen * lax.rsqrt(var + EPS) * lng_ref[...] + lnb_ref[...]
    pa_ref[...] = jnp.concatenate([y * _sigmoid(y), pad], axis=0).astype(pa_ref.dtype)

    u = seg(3) * seg(4)
    rows = []
    for b in range(NB):
        cv = jnp.sum(sb_ref[b] * cb_ref[0:wb - 1, :], axis=0, keepdims=True)
        rows.append(cv + cb_ref[wb - 1:wb, :] * u[b:b + 1, :])
        if wb > 2:
            nsb_ref[b, 0:wb - 2, :] = sb_ref[b, 1:wb - 1, :]
        nsb_ref[b, wb - 2:wb - 1, :] = u[b:b + 1, :]
    pb_ref[...] = jnp.concatenate([seg(2) * jnp.concatenate(rows, axis=0), pad], axis=0).astype(pb_ref.dtype)

    zf = seg(9)
    sg = _sigmoid(zf)
    fg = lb_ref[...] + omlb_ref[...] * sg
    kd = omlb_ref[...] * _sigmoid(-zf)
    zq = seg(8)
    qd = zq * _sigmoid(zq)
    vd = seg(10)
    eye = lax.broadcasted_iota(jnp.int32, (dk, dk), 0) == lax.broadcasted_iota(jnp.int32, (dk, dk), 1)

    def col(x):
        return jnp.sum(jnp.where(eye, x, 0.0), axis=1, keepdims=True)

    for b in range(NB):
        for h in range(H):
            sl = slice(h * dk, (h + 1) * dk)
            s_new = col(fg[b:b + 1, sl]) * sh_ref[b, h] + col(kd[b:b + 1, sl]) * vd[b:b + 1, sl]
            nsh_ref[b, h] = s_new
            o_scr[b:b + 1, sl] = jnp.sum(col(qd[b:b + 1, sl]) * s_new, axis=0, keepdims=True)
    zg = seg(11)
    og = zg * _sigmoid(zg)
    o = o_scr[...]
    outs = [_rms(o[:, h * dk:(h + 1) * dk], gn_ref[...]) for h in range(H)]
    pd_ref[...] = jnp.concatenate([jnp.concatenate(outs, axis=1) * og, pad], axis=0).astype(pd_ref.dtype)


def _sample_mix(proj, sa, sb, sh, dw_a, dw_a_bias, ln_g, ln_b, conv_b, lb, om_lb, g_norm, pre_a, pre_b, pre_d):
    NB, wa1, W = sa.shape
    wb1 = sb.shape[1]
    H, dk = sh.shape[1], sh.shape[2]
    R = proj.shape[0]

    def full(shape):
        return pl.BlockSpec(shape, lambda i: (0,) * len(shape))

    rows = pl.BlockSpec((SUB, W), lambda i: (0, 0))
    return pl.pallas_call(
        functools.partial(_sample_mix_kernel, NB=NB, W=W, H=H, dk=dk, wa=wa1 + 1, wb=wb1 + 1),
        out_shape=(jax.ShapeDtypeStruct((R, W), BF16), jax.ShapeDtypeStruct((R, W), BF16),
                   jax.ShapeDtypeStruct((R, W), BF16), jax.ShapeDtypeStruct(sa.shape, F32),
                   jax.ShapeDtypeStruct(sb.shape, F32), jax.ShapeDtypeStruct(sh.shape, F32)),
        grid=(1,),
        in_specs=[pl.BlockSpec((NB, N_SEG * W), lambda i: (0, 0)), full(sa.shape), full(sb.shape), full(sh.shape),
                  full((wa1 + 1, W)), full((1, W)), full((1, W)), full((1, W)), full((wb1 + 1, W)),
                  full((1, W)), full((1, W)), full((1, dk)),
                  pl.BlockSpec(memory_space=pl.ANY), pl.BlockSpec(memory_space=pl.ANY),
                  pl.BlockSpec(memory_space=pl.ANY)],
        out_specs=(rows, rows, rows, full(sa.shape), full(sb.shape), full(sh.shape)),
        scratch_shapes=[pltpu.VMEM((NB, W), F32)],
        input_output_aliases={12: 0, 13: 1, 14: 2},
        compiler_params=_cparams(("arbitrary",)),
        name="sample_mix",
    )(proj, sa, sb, sh, dw_a, dw_a_bias.reshape(1, W), ln_g.reshape(1, W), ln_b.reshape(1, W), conv_b,
      lb, om_lb, g_norm.reshape(1, dk), pre_a, pre_b, pre_d)


def _decode_kernel(pt_ref, lam_ref, q_ref, kn_ref, vn_ref, bias_ref, biasn_ref, sub_ref, pc_in, *rest,
                   G, P, H, dv, scale, out_scale, NB):
    del pt_ref, pc_in
    k_refs, v_refs = rest[:G], rest[G:2 * G]
    o_ref, m_ref, l_ref, acc_ref, o_scr = rest[2 * G:]
    b = pl.program_id(0)
    s = pl.program_id(1)
    dk = dv // 2
    HM = 2 * H
    W = H * dv

    @pl.when(s == 0)
    def _():
        m_ref[...] = jnp.full(m_ref.shape, NEG, F32)
        l_ref[...] = jnp.zeros(l_ref.shape, F32)
        acc_ref[...] = jnp.zeros(acc_ref.shape, F32)

    q = q_ref[pl.ds(b, 1), :] * scale
    r = lax.broadcasted_iota(jnp.int32, (HM, W), 0)
    c = lax.broadcasted_iota(jnp.int32, (HM, W), 1)
    qbd = jnp.where((c >= r * dk) & (c < (r + 1) * dk), q, 0.0)
    qbd16 = qbd.astype(BF16)

    sc = jnp.concatenate(
        [lax.dot_general(qbd16, k_refs[g][...].astype(BF16), (((1,), (1,)), ((), ())), preferred_element_type=F32)
         for g in range(G)], axis=1) + bias_ref[...]
    m_old = m_ref[...]
    m_new = jnp.maximum(m_old, jnp.max(sc, axis=-1, keepdims=True))
    alpha = jnp.exp(m_old - m_new)
    p = jnp.exp(sc - m_new)
    l_ref[...] = alpha * l_ref[...] + jnp.sum(p, axis=-1, keepdims=True)
    pv = None
    for g in range(G):
        y = jnp.dot(p[:, g * P:(g + 1) * P].astype(BF16), v_refs[g][...].astype(BF16), preferred_element_type=F32)
        pv = y if pv is None else pv + y
    acc_ref[...] = alpha * acc_ref[...] + pv
    m_ref[...] = m_new

    @pl.when(s == pl.num_programs(1) - 1)
    def _():
        kn = kn_ref[pl.ds(b, 1), :]
        vn = vn_ref[pl.ds(b, 1), :]
        sn = jnp.sum(qbd * kn, axis=-1, keepdims=True) + biasn_ref[...]
        m1 = m_ref[...]
        m2 = jnp.maximum(m1, sn)
        a2 = jnp.exp(m1 - m2)
        pn = jnp.exp(sn - m2)
        o = (a2 * acc_ref[...] + pn * vn) / (a2 * l_ref[...] + pn)
        lam = lam_ref[0]
        outs = []
        for h in range(H):
            sl = slice(h * dv, (h + 1) * dv)
            oh = o[2 * h:2 * h + 1, sl] - lam * o[2 * h + 1:2 * h + 2, sl]
            outs.append(_rms(oh, sub_ref[...]) * out_scale)
        o_scr[pl.ds(b, 1), :] = jnp.concatenate(outs, axis=1)

    @pl.when((s == pl.num_programs(1) - 1) & (b == NB - 1))
    def _():
        o_ref[...] = jnp.concatenate([o_scr[...], jnp.zeros((SUB - NB, W), F32)], axis=0).astype(o_ref.dtype)


def _decode_attn(proj, cache_k, cache_v, page_table, layer, bias_past, bias_new, lam, subln, pre_c,
                 *, H, dv, lam_init):
    NB, n_pages = page_table.shape
    P = cache_k.shape[2]
    G = PAGES_PER_STEP
    assert n_pages % G == 0
    W = H * dv
    HM = 2 * H
    ck = cache_k.reshape(cache_k.shape[0], cache_k.shape[1], P, W)
    cv = cache_v.reshape(cache_v.shape[0], cache_v.shape[1], P, W)

    def page(g):
        return pl.BlockSpec((None, None, P, W), lambda b, s, pt: (pt[b, s * G + g], layer, 0, 0))

    def rows(c):
        return pl.BlockSpec((NB, W), lambda b, s, pt: (0, c))

    grid_spec = pltpu.PrefetchScalarGridSpec(
        num_scalar_prefetch=1,
        grid=(NB, n_pages // G),
        in_specs=[pl.BlockSpec(memory_space=pltpu.SMEM), rows(5), rows(6), rows(7),
                  pl.BlockSpec((HM, G * P), lambda b, s, pt: (0, s)),
                  pl.BlockSpec((HM, 1), lambda b, s, pt: (0, 0)),
                  pl.BlockSpec((1, dv), lambda b, s, pt: (0, 0)),
                  pl.BlockSpec(memory_space=pl.ANY)]
        + [page(g) for g in range(G)] + [page(g) for g in range(G)],
        out_specs=pl.BlockSpec((SUB, W), lambda b, s, pt: (0, 0)),
        scratch_shapes=[pltpu.VMEM((HM, 1), F32), pltpu.VMEM((HM, 1), F32), pltpu.VMEM((HM, W), F32),
                        pltpu.VMEM((NB, W), F32)],
    )
    return pl.pallas_call(
        functools.partial(_decode_kernel, G=G, P=P, H=H, dv=dv, scale=(dv // 2) ** -0.5,
                          out_scale=1.0 - lam_init, NB=NB),
        out_shape=jax.ShapeDtypeStruct(pre_c.shape, pre_c.dtype),
        grid_spec=grid_spec,
        input_output_aliases={8: 0},
        compiler_params=_cparams(("arbitrary", "arbitrary")),
        name="decode_attn",
    )(page_table, lam, proj, proj, proj, bias_past, bias_new, subln.reshape(1, dv), pre_c,
      *([ck] * G), *([cv] * G))


def kernel(x_prompt, x_sample, cache_k, cache_v, state_conv_a, state_conv_b, state_hgrn, page_table, meta_tokens, rel_bias_table, hgrn_lower_bound, norm1, w_in, dw_a, dw_a_bias, ln_a_g, ln_a_b, w_a_out, conv_b, w_b_out, lam_q1, lam_k1, lam_q2, lam_k2, subln, w_c_out, g_norm_d, w_d_out, w_gate, b_gate, w_o, norm2, w_up, w_down, final_norm):
    B, L, D = x_prompt.shape
    NB = x_sample.shape[0]
    assert x_sample.shape[1] == 1
    depth = w_in.shape[0]
    n_meta = meta_tokens.shape[0]
    W = state_conv_a.shape[-1]
    H, dv = cache_v.shape[3], cache_v.shape[4]
    Hr, dkr = state_hgrn.shape[2], state_hgrn.shape[3]
    n_buckets = rel_bias_table.shape[0]
    past_len = page_table.shape[1] * cache_k.shape[2]
    assert w_in.shape[2] == N_SEG * W and H * dv == W and Hr * dkr == W and state_conv_b.shape[-1] == W
    assert cache_k.shape[4] == dv and state_hgrn.shape[4] == dkr and dv == LANES and dkr == LANES

    T = SEQ_TILE
    Lt = n_meta + L
    LB = -(-(Lt + NB) // T) * T
    n_pad = LB - Lt
    R = B * LB
    tm = ROW_TILE if R % ROW_TILE == 0 else T
    assert n_pad <= T and NB <= SUB <= n_pad and R % tm == 0

    meta = jnp.broadcast_to(meta_tokens[None].astype(F32), (B, n_meta, D))
    h = jnp.concatenate([jnp.zeros((B, n_pad, D), F32), meta, x_prompt], axis=1)
    h = h.at[0, :NB].set(x_sample[:, 0]).reshape(R, D)

    lbs = jnp.cumsum(jax.nn.softmax(hgrn_lower_bound.astype(F32), axis=0), axis=0)
    lbs = lbs - lbs[0:1]
    bank = _bias_bank(rel_bias_table, T, n_pad)
    pos_s = jnp.full((1,), past_len, jnp.int32)
    bias_dec = jnp.repeat(_rel_bias(pos_s, jnp.arange(past_len + 1), rel_bias_table, n_buckets)[:, 0, :], 2, axis=0)
    bias_past, bias_new = bias_dec[:, :past_len], bias_dec[:, past_len:]

    k_p, v_p, ca_p, cb_p, s_p = [], [], [], [], []
    k_s, v_s, ca_s, cb_s, s_s = [], [], [], [], []
    y = None
    for l in range(depth):
        lam_init = 0.8 - 0.6 * math.exp(-0.3 * l)
        lam = (jnp.exp(jnp.sum(lam_q1[l].astype(F32) * lam_k1[l].astype(F32)))
               - jnp.exp(jnp.sum(lam_q2[l].astype(F32) * lam_k2[l].astype(F32))) + lam_init).reshape(1)
        lb = lbs[l].reshape(1, W)
        log_lb, log1m_lb, om_lb = jnp.log(lb), jnp.log1p(-lb), 1.0 - lb
        w_branch = jnp.stack([w_a_out[l], w_b_out[l], w_c_out[l], w_d_out[l]], axis=0).astype(BF16)

        proj = _matmul(h, w_in[l].astype(BF16), tm=tm, tn=COL_TILE, out_dtype=F32, norm_g=norm1[l], name="proj")
        gates = _matmul(h, w_gate[l].astype(BF16), tm=tm, tn=COL_TILE, out_dtype=BF16, norm_g=norm1[l],
                        bias=b_gate[l], act="sigmoid", name="gates")

        pre_a, pre_b, ca, cb = _conv(proj, dw_a[l], dw_a_bias[l], ln_a_g[l], ln_a_b[l], conv_b[l],
                                     B=B, LB=LB, n_pad=n_pad, W=W)
        pre_c = _attn(proj, bank, lam, subln[l], B=B, LB=LB, H=H, dv=dv, lam_init=lam_init)
        pre_d, s_fin = _hgrn(proj, log_lb, log1m_lb, om_lb, g_norm_d[l], B=B, LB=LB, n_pad=n_pad, H=Hr, dk=dkr)

        pre_a, pre_b, pre_d, nsa, nsb, nsh = _sample_mix(
            proj, state_conv_a[l], state_conv_b[l], state_hgrn[l], dw_a[l], dw_a_bias[l], ln_a_g[l], ln_a_b[l],
            conv_b[l], lb, om_lb, g_norm_d[l], pre_a, pre_b, pre_d)
        pre_c = _decode_attn(proj, cache_k, cache_v, page_table, l, bias_past, bias_new, lam, subln[l], pre_c,
                             H=H, dv=dv, lam_init=lam_init)

        merged = _merge((pre_a, pre_b, pre_c, pre_d), w_branch, gates, tm=tm, tn=COL_TILE)
        h = _matmul(merged, w_o[l].astype(BF16), tm=tm, tn=COL_TILE, out_dtype=F32, residual=h, name="w_o")
        last = l == depth - 1
        h = _mlp(h, norm2[l], w_up[l].astype(BF16), w_down[l].astype(BF16), final_norm if last else None,
                 tm=tm, tf=FF_TILE)

        proj3 = proj.reshape(B, LB, N_SEG * W)
        k_p.append(proj3[:, n_pad:, 6 * W:7 * W])
        v_p.append(proj3[:, n_pad:, 7 * W:8 * W])
        k_s.append(proj[:NB, 6 * W:7 * W])
        v_s.append(proj[:NB, 7 * W:8 * W])
        ca_p.append(ca); cb_p.append(cb); s_p.append(s_fin)
        ca_s.append(nsa); cb_s.append(nsb); s_s.append(nsh)

    y3 = h.reshape(B, LB, D)
    y_prompt = y3[:, n_pad + n_meta:]
    y_sample = y3[0, :NB].reshape(NB, 1, D)
    new_k_prompt = jnp.stack(k_p, axis=2).reshape(B, Lt, depth, H, dv)
    new_v_prompt = jnp.stack(v_p, axis=2).reshape(B, Lt, depth, H, dv)
    new_k_sample = jnp.stack(k_s, axis=1).reshape(NB, 1, depth, H, dv)
    new_v_sample = jnp.stack(v_s, axis=1).reshape(NB, 1, depth, H, dv)
    return (y_prompt, y_sample, new_k_prompt, new_v_prompt, jnp.stack(ca_p, 0), jnp.stack(cb_p, 0),
            jnp.stack(s_p, 0), new_k_sample, new_v_sample, jnp.stack(ca_s, 0), jnp.stack(cb_s, 0),
            jnp.stack(s_s, 0))
```

```python
import functools
import math

import jax
import jax.numpy as jnp
from jax import lax
from jax.experimental import pallas as pl
from jax.experimental.pallas import tpu as pltpu

F32 = jnp.float32
BF16 = jnp.bfloat16

EPS = 1e-6
MAX_DISTANCE = 128
NEG = -1e30
LOG2E = math.log2(math.e)

LANES = 128
SEQ_TILE = 384
ROW_TILE = 768
COL_TILE = 512
FF_TILE = 512
SUB = 16
PAGES_PER_STEP = 8
VMEM_LIMIT = 56 * 1024 * 1024
N_SEG = 12
N_BRANCH = 4


def _cparams(sem):
    return pltpu.CompilerParams(dimension_semantics=sem, vmem_limit_bytes=VMEM_LIMIT)


def _sigmoid(x):
    return 1.0 / (1.0 + jnp.exp(-x))


def _rms(x, g):
    return x * lax.rsqrt(jnp.mean(x * x, axis=-1, keepdims=True) + EPS) * g


def _mm_kernel(*refs, has_norm, has_bias, act, has_res):
    it = iter(refs)
    x_ref = next(it)
    g_ref = next(it) if has_norm else None
    w_ref = next(it)
    b_ref = next(it) if has_bias else None
    r_ref = next(it) if has_res else None
    o_ref = next(it)
    if has_norm:
        xn_ref = next(it)

        @pl.when(pl.program_id(1) == 0)
        def _():
            xn_ref[...] = _rms(x_ref[...], g_ref[...]).astype(BF16)

        lhs = xn_ref[...]
    else:
        lhs = x_ref[...]
    y = jnp.dot(lhs, w_ref[...], preferred_element_type=F32)
    if has_bias:
        y = y + b_ref[...]
    if act == "sigmoid":
        y = _sigmoid(y)
    if has_res:
        y = y + r_ref[...]
    o_ref[...] = y.astype(o_ref.dtype)


def _matmul(x, w, *, tm, tn, out_dtype, norm_g=None, bias=None, act=None, residual=None, name):
    M, K = x.shape
    N = w.shape[1]
    tn = min(tn, N)
    assert M % tm == 0 and N % tn == 0
    args = [x]
    specs = [pl.BlockSpec((tm, K), lambda i, j: (i, 0))]
    scratch = []
    if norm_g is not None:
        args.append(norm_g.reshape(1, K))
        specs.append(pl.BlockSpec((1, K), lambda i, j: (0, 0)))
        scratch.append(pltpu.VMEM((tm, K), BF16))
    args.append(w)
    specs.append(pl.BlockSpec((K, tn), lambda i, j: (0, j)))
    if bias is not None:
        args.append(bias.reshape(1, N))
        specs.append(pl.BlockSpec((1, tn), lambda i, j: (0, j)))
    if residual is not None:
        args.append(residual)
        specs.append(pl.BlockSpec((tm, tn), lambda i, j: (i, j)))
    kern = functools.partial(_mm_kernel, has_norm=norm_g is not None, has_bias=bias is not None,
                             act=act, has_res=residual is not None)
    return pl.pallas_call(
        kern,
        out_shape=jax.ShapeDtypeStruct((M, N), out_dtype),
        grid=(M // tm, N // tn),
        in_specs=specs,
        out_specs=pl.BlockSpec((tm, tn), lambda i, j: (i, j)),
        scratch_shapes=scratch,
        compiler_params=_cparams(("parallel", "arbitrary")),
        name=name,
    )(*args)


def _merge_kernel(pa_ref, pb_ref, pc_ref, pd_ref, w_ref, g0_ref, g1_ref, g2_ref, g3_ref, o_ref):
    acc = None
    for k, (p_ref, g_ref) in enumerate(((pa_ref, g0_ref), (pb_ref, g1_ref), (pc_ref, g2_ref), (pd_ref, g3_ref))):
        y = jnp.dot(p_ref[...], w_ref[k], preferred_element_type=F32) * g_ref[...].astype(F32)
        acc = y if acc is None else acc + y
    o_ref[...] = acc.astype(o_ref.dtype)


def _merge(pres, w_branch, gates, *, tm, tn):
    R, W = pres[0].shape
    D = w_branch.shape[2]
    tn = min(tn, D)
    n_col = D // tn
    pre_spec = pl.BlockSpec((tm, W), lambda i, j: (i, 0))
    gate_specs = [pl.BlockSpec((tm, tn), functools.partial(lambda i, j, b: (i, b * n_col + j), b=b))
                  for b in range(N_BRANCH)]
    return pl.pallas_call(
        _merge_kernel,
        out_shape=jax.ShapeDtypeStruct((R, D), BF16),
        grid=(R // tm, n_col),
        in_specs=[pre_spec] * N_BRANCH + [pl.BlockSpec((N_BRANCH, W, tn), lambda i, j: (0, 0, j))] + gate_specs,
        out_specs=pl.BlockSpec((tm, tn), lambda i, j: (i, j)),
        compiler_params=_cparams(("parallel", "arbitrary")),
        name="merge",
    )(*pres, w_branch, gates, gates, gates, gates)


def _mlp_kernel(*refs, final):
    if final:
        h_ref, g_ref, wu_ref, wd_ref, fn_ref, o_ref, hn_ref, acc_ref = refs
    else:
        h_ref, g_ref, wu_ref, wd_ref, o_ref, hn_ref, acc_ref = refs
    f = pl.program_id(1)

    @pl.when(f == 0)
    def _():
        hn_ref[...] = _rms(h_ref[...], g_ref[...]).astype(BF16)
        acc_ref[...] = h_ref[...]

    u = jnp.dot(hn_ref[...], wu_ref[...], preferred_element_type=F32)
    u = jnp.square(jnp.maximum(u, 0.0))
    acc_ref[...] += jnp.dot(u.astype(BF16), wd_ref[...], preferred_element_type=F32)

    @pl.when(f == pl.num_programs(1) - 1)
    def _():
        y = acc_ref[...]
        if final:
            y = _rms(y, fn_ref[...])
        o_ref[...] = y


def _mlp(h, norm_g, w_up, w_down, final_g, *, tm, tf):
    R, D = h.shape
    FF = w_up.shape[1]
    final = final_g is not None
    args = [h, norm_g.reshape(1, D), w_up, w_down]
    specs = [pl.BlockSpec((tm, D), lambda i, f: (i, 0)),
             pl.BlockSpec((1, D), lambda i, f: (0, 0)),
             pl.BlockSpec((D, tf), lambda i, f: (0, f)),
             pl.BlockSpec((tf, D), lambda i, f: (f, 0))]
    if final:
        args.append(final_g.reshape(1, D))
        specs.append(pl.BlockSpec((1, D), lambda i, f: (0, 0)))
    return pl.pallas_call(
        functools.partial(_mlp_kernel, final=final),
        out_shape=jax.ShapeDtypeStruct((R, D), F32),
        grid=(R // tm, FF // tf),
        in_specs=specs,
        out_specs=pl.BlockSpec((tm, D), lambda i, f: (i, 0)),
        scratch_shapes=[pltpu.VMEM((tm, D), BF16), pltpu.VMEM((tm, D), F32)],
        compiler_params=_cparams(("parallel", "arbitrary")),
        name="mlp",
    )(*args)


def _conv_kernel(au_ref, ag_ref, bb_ref, bc_ref, bh_ref, dwa_ref, dwab_ref, lng_ref, lnb_ref, cb_ref,
                 pa_ref, pb_ref, sa_ref, sb_ref, xa_ref, xb_ref, *, T, n_pad, wa, wb):
    t = pl.program_id(1)
    ha = 32
    hb = 8

    @pl.when(t > 0)
    def _():
        xa_ref[0:ha, :] = xa_ref[T:T + ha, :]
        xb_ref[0:hb, :] = xb_ref[T:T + hb, :]

    xa_ref[ha:ha + T, :] = au_ref[...] * _sigmoid(ag_ref[...])
    xb_ref[hb:hb + T, :] = bc_ref[...] * bh_ref[...]

    @pl.when(t == 0)
    def _():
        xa_ref[0:ha + n_pad, :] = jnp.zeros((ha + n_pad, xa_ref.shape[1]), F32)
        xb_ref[0:hb + n_pad, :] = jnp.zeros((hb + n_pad, xb_ref.shape[1]), F32)

    def chunk(c, carry):
        r0 = pl.multiple_of(c * SUB, SUB)
        cols = []
        for g in range(xa_ref.shape[1] // LANES):
            ls = slice(g * LANES, (g + 1) * LANES)
            win = xa_ref[pl.ds(r0, ha + SUB), ls]
            a = None
            for rho in range(8):
                taps = [j for j in range(wa) if (ha - (wa - 1) + j) % 8 == rho]
                if not taps:
                    continue
                rot = win if rho == 0 else pltpu.roll(win, ha + SUB - rho, 0)
                for j in taps:
                    o = ha - (wa - 1) + j - rho
                    term = dwa_ref[j:j + 1, ls] * rot[o:o + SUB, :]
                    a = term if a is None else a + term
            cols.append(a)
        acc = jnp.concatenate(cols, axis=1) + dwab_ref[...]
        mu = jnp.mean(acc, axis=-1, keepdims=True)
        cen = acc - mu
        var = jnp.mean(cen * cen, axis=-1, keepdims=True)
        y = cen * lax.rsqrt(var + EPS) * lng_ref[...] + lnb_ref[...]
        pa_ref[pl.ds(r0, SUB), :] = (y * _sigmoid(y)).astype(pa_ref.dtype)
        winb = xb_ref[pl.ds(r0, hb + SUB), :]
        cv = None
        for j in range(wb):
            o = hb - (wb - 1) + j
            term = cb_ref[j:j + 1, :] * winb[o:o + SUB, :]
            cv = term if cv is None else cv + term
        pb_ref[pl.ds(r0, SUB), :] = (bb_ref[pl.ds(r0, SUB), :] * cv).astype(pb_ref.dtype)
        return carry

    lax.fori_loop(0, T // SUB, chunk, 0)

    @pl.when(t == pl.num_programs(1) - 1)
    def _():
        sa_ref[0] = xa_ref[T + ha - (wa - 1):T + ha, :]
        sb_ref[0] = xb_ref[T + hb - (wb - 1):T + hb, :]


def _conv(proj, dw_a, dw_a_bias, ln_g, ln_b, conv_b, *, B, LB, n_pad, W):
    T = SEQ_TILE
    nT = LB // T
    wa, wb = dw_a.shape[0], conv_b.shape[0]
    assert wa - 1 <= 32 and wb - 1 <= 8 and n_pad <= T and n_pad % 8 == 0
    R = proj.shape[0]

    def seg(c):
        return pl.BlockSpec((T, W), lambda b, t: (b * nT + t, c))

    def par(n):
        return pl.BlockSpec((n, W), lambda b, t: (0, 0))

    row = pl.BlockSpec((T, W), lambda b, t: (b * nT + t, 0))
    return pl.pallas_call(
        functools.partial(_conv_kernel, T=T, n_pad=n_pad, wa=wa, wb=wb),
        out_shape=(jax.ShapeDtypeStruct((R, W), BF16), jax.ShapeDtypeStruct((R, W), BF16),
                   jax.ShapeDtypeStruct((B, wa - 1, W), F32), jax.ShapeDtypeStruct((B, wb - 1, W), F32)),
        grid=(B, nT),
        in_specs=[seg(0), seg(1), seg(2), seg(3), seg(4), par(wa), par(1), par(1), par(1), par(wb)],
        out_specs=(row, row,
                   pl.BlockSpec((1, wa - 1, W), lambda b, t: (b, 0, 0)),
                   pl.BlockSpec((1, wb - 1, W), lambda b, t: (b, 0, 0))),
        scratch_shapes=[pltpu.VMEM((T + 32, W), F32), pltpu.VMEM((T + 8, W), F32)],
        compiler_params=_cparams(("parallel", "arbitrary")),
        name="conv",
    )(proj, proj, proj, proj, proj, dw_a, dw_a_bias.reshape(1, W), ln_g.reshape(1, W), ln_b.reshape(1, W), conv_b)


B_DIAG, B_SUB, B_00, B_10, B_X0 = 1, 2, 3, 4, 5


def _rel_bias(qpos, kpos, table, n_buckets):
    n = jnp.maximum(qpos[:, None] - kpos[None, :], 0)
    max_exact = n_buckets // 2
    nf = jnp.maximum(n, 1).astype(F32)
    large = max_exact + (jnp.log(nf / max_exact) / math.log(MAX_DISTANCE / max_exact)
                         * (n_buckets - max_exact)).astype(jnp.int32)
    bucket = jnp.where(n < max_exact, n, jnp.minimum(large, n_buckets - 1))
    onehot = (bucket[:, :, None] == jnp.arange(n_buckets)[None, None, :]).astype(F32)
    bias = jnp.einsum("qkn,nh->hqk", onehot, table.astype(F32), precision=lax.Precision.HIGHEST)
    return bias - table[n_buckets - 1].astype(F32)[:, None, None]


def _bias_bank(table, T, n_pad):
    nb, H = table.shape
    r = jnp.arange(T)
    causal = (r[None, :] <= r[:, None])[None]
    kpad = (r < n_pad)[None, None, :]
    qval = (r >= n_pad)[None, :, None]
    zero = jnp.zeros((H, T, T), F32)
    diag = jnp.where(causal, _rel_bias(r, r, table, nb), NEG)
    sub = _rel_bias(r + T, r, table, nb)
    b00 = jnp.where(kpad & qval, NEG, diag)
    b10 = jnp.where(kpad, NEG, sub)
    bx0 = jnp.where(kpad, NEG, zero)
    return jnp.stack([zero, diag, sub, b00, b10, bx0], axis=0) * LOG2E


def _sublane_all(x, op):
    for sh in (4, 2, 1):
        x = op(x, pltpu.roll(x, sh, 0))
    return x


def _attn_kernel(lam_ref, q_ref, k_ref, v_ref, bias_ref, sub_ref, o_ref, qm_ref, k16_ref, vt_ref, m_ref, l_ref,
                 acc_ref, *, T, LB, dk, dv, scale, out_scale):
    i = pl.program_id(2)
    S8 = 8

    @pl.when(i == 0)
    def _():
        k16_ref[...] = k_ref[...].astype(BF16)
        for c in range(LB // LANES):
            r0 = c * LANES
            vt_ref[r0 // T, :, r0 % T:r0 % T + LANES] = v_ref[r0:r0 + LANES, :].T.astype(BF16)

    q = q_ref[...] * (scale * LOG2E)
    lane = lax.broadcasted_iota(jnp.int32, q.shape, 1)
    qm_ref[0] = jnp.where(lane < dk, q, 0.0).astype(BF16)
    qm_ref[1] = jnp.where(lane >= dk, q, 0.0).astype(BF16)

    m_ref[...] = jnp.full(m_ref.shape, NEG, F32)
    l_ref[...] = jnp.zeros(l_ref.shape, F32)
    acc_ref[...] = jnp.zeros(acc_ref.shape, F32)

    def tile(j, bidx, nt=1):
        r0 = pl.multiple_of(j * T, T)
        k = k16_ref[pl.ds(r0, nt * T), :]
        for mp in range(2):
            s = lax.dot_general(k, qm_ref[mp], (((1,), (1,)), ((), ())), preferred_element_type=F32)
            if bidx is not None:
                s = s + bias_ref[bidx, 0]
            s3 = s.reshape(nt * T // S8, S8, T)
            m_old = m_ref[mp]
            m_new = jnp.maximum(m_old, _sublane_all(jnp.max(s3, axis=0), jnp.maximum))
            alpha = jnp.exp2(m_old - m_new)
            p3 = jnp.exp2(s3 - m_new[None])
            l_ref[mp] = alpha * l_ref[mp] + jnp.sum(p3, axis=0)
            p = p3.reshape(nt * T, T).astype(BF16)
            pv = jnp.dot(vt_ref[j], p[0:T], preferred_element_type=F32)
            for u in range(1, nt):
                pv = pv + jnp.dot(vt_ref[j + u], p[u * T:(u + 1) * T], preferred_element_type=F32)
            acc_ref[mp] = alpha[None] * acc_ref[mp] + pv.reshape(dv // S8, S8, T)
            m_ref[mp] = m_new

    tile(0, jnp.where(i == 0, B_00, jnp.where(i == 1, B_10, B_X0)))

    n_far = jnp.maximum(i - 2, 0)

    def far(jj, carry):
        tile(1 + 2 * jj, None, nt=2)
        return carry

    lax.fori_loop(0, n_far // 2, far, 0)

    @pl.when(n_far % 2 == 1)
    def _():
        tile(i - 2, None)

    @pl.when(i >= 2)
    def _():
        tile(i - 1, B_SUB)

    @pl.when(i >= 1)
    def _():
        tile(i, B_DIAG)

    lam = lam_ref[0]
    l0 = _sublane_all(l_ref[0], jnp.add)
    l1 = _sublane_all(l_ref[1], jnp.add)
    o3 = acc_ref[0] / l0[None] - lam * (acc_ref[1] / l1[None])
    ms = _sublane_all(jnp.sum(o3 * o3, axis=0), jnp.add) * (1.0 / dv)
    y = (o3 * lax.rsqrt(ms + EPS)[None]).reshape(dv, T).T
    o_ref[...] = (y * sub_ref[...] * out_scale).astype(o_ref.dtype)


def _attn(proj, bank, lam, subln, *, B, LB, H, dv, lam_init):
    T = SEQ_TILE
    nQ = LB // T
    R = proj.shape[0]
    dk = dv // 2
    qc, kc, vc = 5 * H, 6 * H, 7 * H
    return pl.pallas_call(
        functools.partial(_attn_kernel, T=T, LB=LB, dk=dk, dv=dv, scale=dk ** -0.5, out_scale=1.0 - lam_init),
        out_shape=jax.ShapeDtypeStruct((R, H * dv), BF16),
        grid=(B, H, nQ),
        in_specs=[pl.BlockSpec(memory_space=pltpu.SMEM),
                  pl.BlockSpec((T, dv), lambda b, h, i: (b * nQ + i, qc + h)),
                  pl.BlockSpec((LB, dv), lambda b, h, i: (b, kc + h)),
                  pl.BlockSpec((LB, dv), lambda b, h, i: (b, vc + h)),
                  pl.BlockSpec((6, 1, T, T), lambda b, h, i: (0, h, 0, 0)),
                  pl.BlockSpec((1, dv), lambda b, h, i: (0, 0))],
        out_specs=pl.BlockSpec((T, dv), lambda b, h, i: (b * nQ + i, h)),
        scratch_shapes=[pltpu.VMEM((2, T, dv), BF16), pltpu.VMEM((LB, dv), BF16), pltpu.VMEM((nQ, dv, T), BF16),
                        pltpu.VMEM((2, 8, T), F32), pltpu.VMEM((2, 8, T), F32), pltpu.VMEM((2, dv // 8, 8, T), F32)],
        compiler_params=_cparams(("parallel", "parallel", "arbitrary")),
        name="attn",
    )(lam, proj, proj, proj, jnp.swapaxes(bank, -1, -2), subln.reshape(1, dv))


def _log_forget(zf, log_lb, log1m_lb):
    ls = jnp.minimum(zf, 0.0) - jnp.log1p(jnp.exp(-jnp.abs(zf)))
    b = log1m_lb + ls
    hi = jnp.maximum(log_lb, b)
    lo = jnp.minimum(log_lb, b)
    return hi + jnp.log1p(jnp.exp(lo - hi))


def _split3(x):
    hi = x.astype(BF16)
    r = x - hi.astype(F32)
    mid = r.astype(BF16)
    lo = (r - mid.astype(F32)).astype(BF16)
    return hi, mid, lo


def _hgrn_kernel(q_ref, f_ref, i_ref, g_ref, loglb_ref, log1m_ref, omlb_ref, gn_ref,
                 o_ref, s_ref, st_ref, *, T, n_pad, H, dk):
    t = pl.program_id(1)

    @pl.when(t == 0)
    def _():
        st_ref[...] = jnp.zeros(st_ref.shape, F32)

    rr = lax.broadcasted_iota(jnp.int32, (SUB, SUB), 0)
    cc = lax.broadcasted_iota(jnp.int32, (SUB, SUB), 1)
    tri = jnp.where(cc <= rr, 1.0, 0.0).astype(BF16)
    row = lax.broadcasted_iota(jnp.int32, (SUB, 1), 0)

    def chunk(c, carry):
        r0 = pl.multiple_of(c * SUB, SUB)
        zf = f_ref[pl.ds(r0, SUB), :]
        logf = _log_forget(zf, loglb_ref[...], log1m_ref[...])
        hi, mid, lo = _split3(logf)
        bcum = (jnp.dot(tri, hi, preferred_element_type=F32) + jnp.dot(tri, mid, preferred_element_type=F32)
                + jnp.dot(tri, lo, preferred_element_type=F32))
        kd = omlb_ref[...] * _sigmoid(-zf)
        zq = q_ref[pl.ds(r0, SUB), :]
        qd = zq * _sigmoid(zq)
        seq_row = t * T + r0 + row
        vd = jnp.where(seq_row >= n_pad, i_ref[pl.ds(r0, SUB), :], 0.0)
        zg = g_ref[pl.ds(r0, SUB), :]
        og = zg * _sigmoid(zg)
        for h in range(H):
            sl = slice(h * dk, (h + 1) * dk)
            b, qh, kh, vh = bcum[:, sl], qd[:, sl], kd[:, sl], vd[:, sl]
            blast = b[SUB - 1:SUB, :]
            st = st_ref[h]
            o = lax.dot_general((qh * jnp.exp(b)).astype(BF16), st.astype(BF16),
                                (((1,), (1,)), ((), ())), preferred_element_type=F32)
            for d in range(SUB):
                if d == 0:
                    kr, br, vr = kh, b, vh
                else:
                    kr, br, vr = pltpu.roll(kh, d, 0), pltpu.roll(b, d, 0), pltpu.roll(vh, d, 0)
                w = jnp.where(row >= d, qh * kr * jnp.exp(b - br), 0.0)
                o = o + jnp.sum(w, axis=-1, keepdims=True) * vr
            kt = (kh * jnp.exp(blast - b)).astype(BF16)
            kv = lax.dot_general(vh.astype(BF16), kt, (((0,), (0,)), ((), ())), preferred_element_type=F32)
            st_ref[h] = st * jnp.exp(blast) + kv
            o_ref[pl.ds(r0, SUB), sl] = (_rms(o, gn_ref[...]) * og[:, sl]).astype(o_ref.dtype)
        return carry

    lax.fori_loop(0, T // SUB, chunk, 0)

    @pl.when(t == pl.num_programs(1) - 1)
    def _():
        for h in range(H):
            s_ref[0, h] = st_ref[h].T


def _hgrn(proj, log_lb, log1m_lb, om_lb, g_norm, *, B, LB, n_pad, H, dk):
    T = SEQ_TILE
    nT = LB // T
    R = proj.shape[0]
    W = H * dk

    def seg(c):
        return pl.BlockSpec((T, W), lambda b, t: (b * nT + t, c))

    def par(n):
        return pl.BlockSpec((1, n), lambda b, t: (0, 0))

    return pl.pallas_call(
        functools.partial(_hgrn_kernel, T=T, n_pad=n_pad, H=H, dk=dk),
        out_shape=(jax.ShapeDtypeStruct((R, W), BF16), jax.ShapeDtypeStruct((B, H, dk, dk), F32)),
        grid=(B, nT),
        in_specs=[seg(8), seg(9), seg(10), seg(11), par(W), par(W), par(W), par(dk)],
        out_specs=(pl.BlockSpec((T, W), lambda b, t: (b * nT + t, 0)),
                   pl.BlockSpec((1, H, dk, dk), lambda b, t: (b, 0, 0, 0))),
        scratch_shapes=[pltpu.VMEM((H, dk, dk), F32)],
        compiler_params=_cparams(("parallel", "arbitrary")),
        name="hgrn",
    )(proj, proj, proj, proj, log_lb, log1m_lb, om_lb, g_norm.reshape(1, dk))


def _sample_mix_kernel(p_ref, sa_ref, sb_ref, sh_ref, dwa_ref, dwab_ref, lng_ref, lnb_ref, cb_ref,
                       lb_ref, omlb_ref, gn_ref, pa_in, pb_in, pd_in,
                       pa_ref, pb_ref, pd_ref, nsa_ref, nsb_ref, nsh_ref, o_scr, *, NB, W, H, dk, wa, wb):
    del pa_in, pb_in, pd_in

    def seg(c):
        return p_ref[:, c * W:(c + 1) * W]

    pad = jnp.zeros((SUB - NB, W), F32)

    glu = seg(0) * _sigmoid(seg(1))
    rows = []
    for b in range(NB):
        cv = jnp.sum(sa_ref[b] * dwa_ref[0:wa - 1, :], axis=0, keepdims=True)
        rows.append(cv + dwa_ref[wa - 1:wa, :] * glu[b:b + 1, :])
        nsa_ref[b, 0:wa - 2, :] = sa_ref[b, 1:wa - 1, :]
        nsa_ref[b, wa - 2:wa - 1, :] = glu[b:b + 1, :]
    acc = jnp.concatenate(rows, axis=0) + dwab_ref[...]
    mu = jnp.mean(acc, axis=-1, keepdims=True)
    cen = acc - mu
    var = jnp.mean(cen * cen, axis=-1, keepdims=True)
    y = cen * lax.rsqrt(var + EPS) * lng_ref[...] + lnb_ref[...]
    pa_ref[...] = jnp.concatenate([y * _sigmoid(y), pad], axis=0).astype(pa_ref.dtype)

    u = seg(3) * seg(4)
    rows = []
    for b in range(NB):
        cv = jnp.sum(sb_ref[b] * cb_ref[0:wb - 1, :], axis=0, keepdims=True)
        rows.append(cv + cb_ref[wb - 1:wb, :] * u[b:b + 1, :])
        if wb > 2:
            nsb_ref[b, 0:wb - 2, :] = sb_ref[b, 1:wb - 1, :]
        nsb_ref[b, wb - 2:wb - 1, :] = u[b:b + 1, :]
    pb_ref[...] = jnp.concatenate([seg(2) * jnp.concatenate(rows, axis=0), pad], axis=0).astype(pb_ref.dtype)

    zf = seg(9)
    sg = _sigmoid(zf)
    fg = lb_ref[...] + omlb_ref[...] * sg
    kd = omlb_ref[...] * _sigmoid(-zf)
    zq = seg(8)
    qd = zq * _sigmoid(zq)
    vd = seg(10)
    eye = lax.broadcasted_iota(jnp.int32, (dk, dk), 0) == lax.broadcasted_iota(jnp.int32, (dk, dk), 1)

    def col(x):
        return jnp.sum(jnp.where(eye, x, 0.0), axis=1, keepdims=True)

    for b in range(NB):
        for h in range(H):
            sl = slice(h * dk, (h + 1) * dk)
            s_new = col(fg[b:b + 1, sl]) * sh_ref[b, h] + col(kd[b:b + 1, sl]) * vd[b:b + 1, sl]
            nsh_ref[b, h] = s_new
            o_scr[b:b + 1, sl] = jnp.sum(col(qd[b:b + 1, sl]) * s_new, axis=0, keepdims=True)
    zg = seg(11)
    og = zg * _sigmoid(zg)
    o = o_scr[...]
    outs = [_rms(o[:, h * dk:(h + 1) * dk], gn_ref[...]) for h in range(H)]
    pd_ref[...] = jnp.concatenate([jnp.concatenate(outs, axis=1) * og, pad], axis=0).astype(pd_ref.dtype)


def _sample_mix(proj, sa, sb, sh, dw_a, dw_a_bias, ln_g, ln_b, conv_b, lb, om_lb, g_norm, pre_a, pre_b, pre_d):
    NB, wa1, W = sa.shape
    wb1 = sb.shape[1]
    H, dk = sh.shape[1], sh.shape[2]
    R = proj.shape[0]

    def full(shape):
        return pl.BlockSpec(shape, lambda i: (0,) * len(shape))

    rows = pl.BlockSpec((SUB, W), lambda i: (0, 0))
    return pl.pallas_call(
        functools.partial(_sample_mix_kernel, NB=NB, W=W, H=H, dk=dk, wa=wa1 + 1, wb=wb1 + 1),
        out_shape=(jax.ShapeDtypeStruct((R, W), BF16), jax.ShapeDtypeStruct((R, W), BF16),
                   jax.ShapeDtypeStruct((R, W), BF16), jax.ShapeDtypeStruct(sa.shape, F32),
                   jax.ShapeDtypeStruct(sb.shape, F32), jax.ShapeDtypeStruct(sh.shape, F32)),
        grid=(1,),
        in_specs=[pl.BlockSpec((NB, N_SEG * W), lambda i: (0, 0)), full(sa.shape), full(sb.shape), full(sh.shape),
                  full((wa1 + 1, W)), full((1, W)), full((1, W)), full((1, W)), full((wb1 + 1, W)),
                  full((1, W)), full((1, W)), full((1, dk)),
                  pl.BlockSpec(memory_space=pl.ANY), pl.BlockSpec(memory_space=pl.ANY),
                  pl.BlockSpec(memory_space=pl.ANY)],
        out_specs=(rows, rows, rows, full(sa.shape), full(sb.shape), full(sh.shape)),
        scratch_shapes=[pltpu.VMEM((NB, W), F32)],
        input_output_aliases={12: 0, 13: 1, 14: 2},
        compiler_params=_cparams(("arbitrary",)),
        name="sample_mix",
    )(proj, sa, sb, sh, dw_a, dw_a_bias.reshape(1, W), ln_g.reshape(1, W), ln_b.reshape(1, W), conv_b,
      lb, om_lb, g_norm.reshape(1, dk), pre_a, pre_b, pre_d)


def _decode_kernel(pt_ref, lam_ref, q_ref, kn_ref, vn_ref, bias_ref, biasn_ref, sub_ref, pc_in, *rest,
                   G, P, H, dv, scale, out_scale, NB):
    del pt_ref, pc_in
    k_refs, v_refs = rest[:G], rest[G:2 * G]
    o_ref, m_ref, l_ref, acc_ref, o_scr = rest[2 * G:]
    b = pl.program_id(0)
    s = pl.program_id(1)
    dk = dv // 2
    HM = 2 * H
    W = H * dv
    C = P * H
    last = s == pl.num_programs(1) - 1

    @pl.when(s == 0)
    def _():
        m_ref[...] = jnp.full(m_ref.shape, NEG, F32)
        l_ref[...] = jnp.zeros(l_ref.shape, F32)
        acc_ref[...] = jnp.zeros(acc_ref.shape, F32)

    def per_map_rows(row):
        return jnp.concatenate([row[:, h * dv:(h + 1) * dv] for h in range(H) for _ in range(2)], axis=0)

    rr = lax.broadcasted_iota(jnp.int32, (HM, dv), 0)
    ll = lax.broadcasted_iota(jnp.int32, (HM, dv), 1)
    own_map = (ll >= dk) == (jnp.bitwise_and(rr, 1) == 1)
    qall = jnp.where(own_map, per_map_rows(q_ref[pl.ds(b, 1), :]) * (scale * LOG2E), 0.0)
    qall16 = qall.astype(BF16)
    col_head = jnp.bitwise_and(lax.broadcasted_iota(jnp.int32, (HM, C), 1), H - 1)
    row_head = lax.shift_right_logical(lax.broadcasted_iota(jnp.int32, (HM, C), 0), 1)
    own_head = col_head == row_head

    blocks = []
    for g in range(G):
        kg = k_refs[g][...].reshape(C, dv).astype(BF16)
        sg = lax.dot_general(qall16, kg, (((1,), (1,)), ((), ())), preferred_element_type=F32)
        if g == G - 1:
            sg = sg + jnp.where(last, bias_ref[...], 0.0)
        blocks.append(jnp.where(own_head, sg, NEG))
    sc = jnp.concatenate(blocks, axis=1)
    m_old = m_ref[...]
    m_new = jnp.maximum(m_old, jnp.max(sc, axis=-1, keepdims=True))
    alpha = jnp.exp2(m_old - m_new)
    p = jnp.exp2(sc - m_new)
    l_ref[...] = alpha * l_ref[...] + jnp.sum(p, axis=-1, keepdims=True)
    pv = None
    for g in range(G):
        vg = v_refs[g][...].reshape(C, dv).astype(BF16)
        y = jnp.dot(p[:, g * C:(g + 1) * C].astype(BF16), vg, preferred_element_type=F32)
        pv = y if pv is None else pv + y
    acc_ref[...] = alpha * acc_ref[...] + pv
    m_ref[...] = m_new

    @pl.when(last)
    def _():
        kn = per_map_rows(kn_ref[pl.ds(b, 1), :])
        vn = per_map_rows(vn_ref[pl.ds(b, 1), :])
        sn = jnp.sum(qall * kn, axis=-1, keepdims=True) + biasn_ref[...]
        m1 = m_ref[...]
        m2 = jnp.maximum(m1, sn)
        a2 = jnp.exp2(m1 - m2)
        pn = jnp.exp2(sn - m2)
        o = (a2 * acc_ref[...] + pn * vn) / (a2 * l_ref[...] + pn)
        lam = lam_ref[0]
        outs = []
        for h in range(H):
            oh = o[2 * h:2 * h + 1, :] - lam * o[2 * h + 1:2 * h + 2, :]
            outs.append(_rms(oh, sub_ref[...]) * out_scale)
        o_scr[pl.ds(b, 1), :] = jnp.concatenate(outs, axis=1)

    @pl.when((s == pl.num_programs(1) - 1) & (b == NB - 1))
    def _():
        o_ref[...] = jnp.concatenate([o_scr[...], jnp.zeros((SUB - NB, W), F32)], axis=0).astype(o_ref.dtype)


def _decode_attn(proj, cache_k, cache_v, page_table, layer, bias_past, bias_new, lam, subln, pre_c,
                 *, H, dv, lam_init):
    NB, n_pages = page_table.shape
    P = cache_k.shape[2]
    G = PAGES_PER_STEP
    assert n_pages % G == 0 and P >= MAX_DISTANCE and H & (H - 1) == 0
    W = H * dv
    HM = 2 * H

    def page(g):
        return pl.BlockSpec((None, None, P, H, dv), lambda b, s, pt: (pt[b, s * G + g], layer, 0, 0, 0))

    def rows(c):
        return pl.BlockSpec((NB, W), lambda b, s, pt: (0, c))

    grid_spec = pltpu.PrefetchScalarGridSpec(
        num_scalar_prefetch=1,
        grid=(NB, n_pages // G),
        in_specs=[pl.BlockSpec(memory_space=pltpu.SMEM), rows(5), rows(6), rows(7),
                  pl.BlockSpec((HM, P * H), lambda b, s, pt: (0, 0)),
                  pl.BlockSpec((HM, 1), lambda b, s, pt: (0, 0)),
                  pl.BlockSpec((1, dv), lambda b, s, pt: (0, 0)),
                  pl.BlockSpec(memory_space=pl.ANY)]
        + [page(g) for g in range(G)] + [page(g) for g in range(G)],
        out_specs=pl.BlockSpec((SUB, W), lambda b, s, pt: (0, 0)),
        scratch_shapes=[pltpu.VMEM((HM, 1), F32), pltpu.VMEM((HM, 1), F32), pltpu.VMEM((HM, dv), F32),
                        pltpu.VMEM((NB, W), F32)],
    )
    return pl.pallas_call(
        functools.partial(_decode_kernel, G=G, P=P, H=H, dv=dv, scale=(dv // 2) ** -0.5,
                          out_scale=1.0 - lam_init, NB=NB),
        out_shape=jax.ShapeDtypeStruct(pre_c.shape, pre_c.dtype),
        grid_spec=grid_spec,
        input_output_aliases={8: 0},
        compiler_params=_cparams(("arbitrary", "arbitrary")),
        name="decode_attn",
    )(page_table, lam, proj, proj, proj, bias_past, bias_new, subln.reshape(1, dv), pre_c,
      *([cache_k] * G), *([cache_v] * G))


def kernel(x_prompt, x_sample, cache_k, cache_v, state_conv_a, state_conv_b, state_hgrn, page_table, meta_tokens, rel_bias_table, hgrn_lower_bound, norm1, w_in, dw_a, dw_a_bias, ln_a_g, ln_a_b, w_a_out, conv_b, w_b_out, lam_q1, lam_k1, lam_q2, lam_k2, subln, w_c_out, g_norm_d, w_d_out, w_gate, b_gate, w_o, norm2, w_up, w_down, final_norm):
    B, L, D = x_prompt.shape
    NB = x_sample.shape[0]
    assert x_sample.shape[1] == 1
    depth = w_in.shape[0]
    n_meta = meta_tokens.shape[0]
    W = state_conv_a.shape[-1]
    H, dv = cache_v.shape[3], cache_v.shape[4]
    Hr, dkr = state_hgrn.shape[2], state_hgrn.shape[3]
    n_buckets = rel_bias_table.shape[0]
    past_len = page_table.shape[1] * cache_k.shape[2]
    assert w_in.shape[2] == N_SEG * W and H * dv == W and Hr * dkr == W and state_conv_b.shape[-1] == W
    assert cache_k.shape[4] == dv and state_hgrn.shape[4] == dkr and dv == LANES and dkr == LANES

    T = SEQ_TILE
    Lt = n_meta + L
    LB = -(-(Lt + NB) // T) * T
    n_pad = LB - Lt
    R = B * LB
    tm = ROW_TILE if R % ROW_TILE == 0 else T
    assert n_pad <= T and NB <= SUB <= n_pad and R % tm == 0

    meta = jnp.broadcast_to(meta_tokens[None].astype(F32), (B, n_meta, D))
    h = jnp.concatenate([jnp.zeros((B, n_pad, D), F32), meta, x_prompt], axis=1)
    h = h.at[0, :NB].set(x_sample[:, 0]).reshape(R, D)

    lbs = jnp.cumsum(jax.nn.softmax(hgrn_lower_bound.astype(F32), axis=0), axis=0)
    lbs = lbs - lbs[0:1]
    bank = _bias_bank(rel_bias_table, T, n_pad)
    pos_s = jnp.full((1,), past_len, jnp.int32)
    P = cache_k.shape[2]
    kpos_near = jnp.arange(past_len - P, past_len + 1)
    bias_dec = jnp.repeat(_rel_bias(pos_s, kpos_near, rel_bias_table, n_buckets)[:, 0, :], 2, axis=0) * LOG2E
    bias_past, bias_new = jnp.repeat(bias_dec[:, :P], H, axis=1), bias_dec[:, P:]

    k_p, v_p, ca_p, cb_p, s_p = [], [], [], [], []
    k_s, v_s, ca_s, cb_s, s_s = [], [], [], [], []
    y = None
    for l in range(depth):
        lam_init = 0.8 - 0.6 * math.exp(-0.3 * l)
        lam = (jnp.exp(jnp.sum(lam_q1[l].astype(F32) * lam_k1[l].astype(F32)))
               - jnp.exp(jnp.sum(lam_q2[l].astype(F32) * lam_k2[l].astype(F32))) + lam_init).reshape(1)
        lb = lbs[l].reshape(1, W)
        log_lb, log1m_lb, om_lb = jnp.log(lb), jnp.log1p(-lb), 1.0 - lb
        w_branch = jnp.stack([w_a_out[l], w_b_out[l], w_c_out[l], w_d_out[l]], axis=0).astype(BF16)

        proj = _matmul(h, w_in[l].astype(BF16), tm=tm, tn=COL_TILE, out_dtype=F32, norm_g=norm1[l], name="proj")
        gates = _matmul(h, w_gate[l].astype(BF16), tm=tm, tn=COL_TILE, out_dtype=BF16, norm_g=norm1[l],
                        bias=b_gate[l], act="sigmoid", name="gates")

        pre_a, pre_b, ca, cb = _conv(proj, dw_a[l], dw_a_bias[l], ln_a_g[l], ln_a_b[l], conv_b[l],
                                     B=B, LB=LB, n_pad=n_pad, W=W)
        pre_c = _attn(proj, bank, lam, subln[l], B=B, LB=LB, H=H, dv=dv, lam_init=lam_init)
        pre_d, s_fin = _hgrn(proj, log_lb, log1m_lb, om_lb, g_norm_d[l], B=B, LB=LB, n_pad=n_pad, H=Hr, dk=dkr)

        pre_a, pre_b, pre_d, nsa, nsb, nsh = _sample_mix(
            proj, state_conv_a[l], state_conv_b[l], state_hgrn[l], dw_a[l], dw_a_bias[l], ln_a_g[l], ln_a_b[l],
            conv_b[l], lb, om_lb, g_norm_d[l], pre_a, pre_b, pre_d)
        pre_c = _decode_attn(proj, cache_k, cache_v, page_table, l, bias_past, bias_new, lam, subln[l], pre_c,
                             H=H, dv=dv, lam_init=lam_init)

        merged = _merge((pre_a, pre_b, pre_c, pre_d), w_branch, gates, tm=tm, tn=COL_TILE)
        h = _matmul(merged, w_o[l].astype(BF16), tm=tm, tn=COL_TILE, out_dtype=F32, residual=h, name="w_o")
        last = l == depth - 1
        h = _mlp(h, norm2[l], w_up[l].astype(BF16), w_down[l].astype(BF16), final_norm if last else None,
                 tm=tm, tf=FF_TILE)

        proj3 = proj.reshape(B, LB, N_SEG * W)
        k_p.append(proj3[:, n_pad:, 6 * W:7 * W])
        v_p.append(proj3[:, n_pad:, 7 * W:8 * W])
        k_s.append(proj[:NB, 6 * W:7 * W])
        v_s.append(proj[:NB, 7 * W:8 * W])
        ca_p.append(ca); cb_p.append(cb); s_p.append(s_fin)
        ca_s.append(nsa); cb_s.append(nsb); s_s.append(nsh)

    y3 = h.reshape(B, LB, D)
    y_prompt = y3[:, n_pad + n_meta:]
    y_sample = y3[0, :NB].reshape(NB, 1, D)
    new_k_prompt = jnp.stack(k_p, axis=2).reshape(B, Lt, depth, H, dv)
    new_v_prompt = jnp.stack(v_p, axis=2).reshape(B, Lt, depth, H, dv)
    new_k_sample = jnp.stack(k_s, axis=1).reshape(NB, 1, depth, H, dv)
    new_v_sample = jnp.stack(v_s, axis=1).reshape(NB, 1, depth, H, dv)
    return (y_prompt, y_sample, new_k_prompt, new_v_prompt, jnp.stack(ca_p, 0), jnp.stack(cb_p, 0),
            jnp.stack(s_p, 0), new_k_sample, new_v_sample, jnp.stack(ca_s, 0), jnp.stack(cb_s, 0),
            jnp.stack(s_s, 0))
```

```python
import functools
import math

import jax
import jax.numpy as jnp
from jax import lax
from jax.experimental import pallas as pl
from jax.experimental.pallas import tpu as pltpu

F32 = jnp.float32
BF16 = jnp.bfloat16

EPS = 1e-6
MAX_DISTANCE = 128
NEG = -1e30
LOG2E = math.log2(math.e)

LANES = 128
SEQ_TILE = 384
ROW_TILE = 768
COL_TILE = 1024
MERGE_COL_TILE = 512
FF_TILE = 1024
SUB = 16
PAGES_PER_STEP = 8
VMEM_LIMIT = 56 * 1024 * 1024
N_SEG = 12
N_BRANCH = 4


def _cparams(sem):
    return pltpu.CompilerParams(dimension_semantics=sem, vmem_limit_bytes=VMEM_LIMIT)


def _sigmoid(x):
    return 1.0 / (1.0 + jnp.exp(-x))


def _rms(x, g):
    return x * lax.rsqrt(jnp.mean(x * x, axis=-1, keepdims=True) + EPS) * g


def _mm_kernel(*refs, has_norm, has_bias, act, has_res):
    it = iter(refs)
    x_ref = next(it)
    g_ref = next(it) if has_norm else None
    w_ref = next(it)
    b_ref = next(it) if has_bias else None
    r_ref = next(it) if has_res else None
    o_ref = next(it)
    if has_norm:
        xn_ref = next(it)

        @pl.when(pl.program_id(1) == 0)
        def _():
            xn_ref[...] = _rms(x_ref[...], g_ref[...]).astype(BF16)

        lhs = xn_ref[...]
    else:
        lhs = x_ref[...]
    y = jnp.dot(lhs, w_ref[...], preferred_element_type=F32)
    if has_bias:
        y = y + b_ref[...]
    if act == "sigmoid":
        y = _sigmoid(y)
    if has_res:
        y = y + r_ref[...]
    o_ref[...] = y.astype(o_ref.dtype)


def _matmul(x, w, *, tm, tn, out_dtype, norm_g=None, bias=None, act=None, residual=None, name):
    M, K = x.shape
    N = w.shape[1]
    tn = min(tn, N)
    assert M % tm == 0 and N % tn == 0
    args = [x]
    specs = [pl.BlockSpec((tm, K), lambda i, j: (i, 0))]
    scratch = []
    if norm_g is not None:
        args.append(norm_g.reshape(1, K))
        specs.append(pl.BlockSpec((1, K), lambda i, j: (0, 0)))
        scratch.append(pltpu.VMEM((tm, K), BF16))
    args.append(w)
    specs.append(pl.BlockSpec((K, tn), lambda i, j: (0, j)))
    if bias is not None:
        args.append(bias.reshape(1, N))
        specs.append(pl.BlockSpec((1, tn), lambda i, j: (0, j)))
    if residual is not None:
        args.append(residual)
        specs.append(pl.BlockSpec((tm, tn), lambda i, j: (i, j)))
    kern = functools.partial(_mm_kernel, has_norm=norm_g is not None, has_bias=bias is not None,
                             act=act, has_res=residual is not None)
    return pl.pallas_call(
        kern,
        out_shape=jax.ShapeDtypeStruct((M, N), out_dtype),
        grid=(M // tm, N // tn),
        in_specs=specs,
        out_specs=pl.BlockSpec((tm, tn), lambda i, j: (i, j)),
        scratch_shapes=scratch,
        compiler_params=_cparams(("parallel", "arbitrary")),
        name=name,
    )(*args)


def _merge_kernel(pa_ref, pb_ref, pc_ref, pd_ref, w_ref, g0_ref, g1_ref, g2_ref, g3_ref, o_ref):
    acc = None
    for k, (p_ref, g_ref) in enumerate(((pa_ref, g0_ref), (pb_ref, g1_ref), (pc_ref, g2_ref), (pd_ref, g3_ref))):
        y = jnp.dot(p_ref[...], w_ref[k], preferred_element_type=F32) * g_ref[...].astype(F32)
        acc = y if acc is None else acc + y
    o_ref[...] = acc.astype(o_ref.dtype)


def _merge(pres, w_branch, gates, *, tm, tn):
    R, W = pres[0].shape
    D = w_branch.shape[2]
    tn = min(tn, D)
    n_col = D // tn
    pre_spec = pl.BlockSpec((tm, W), lambda i, j: (i, 0))
    gate_specs = [pl.BlockSpec((tm, tn), functools.partial(lambda i, j, b: (i, b * n_col + j), b=b))
                  for b in range(N_BRANCH)]
    return pl.pallas_call(
        _merge_kernel,
        out_shape=jax.ShapeDtypeStruct((R, D), BF16),
        grid=(R // tm, n_col),
        in_specs=[pre_spec] * N_BRANCH + [pl.BlockSpec((N_BRANCH, W, tn), lambda i, j: (0, 0, j))] + gate_specs,
        out_specs=pl.BlockSpec((tm, tn), lambda i, j: (i, j)),
        compiler_params=_cparams(("parallel", "arbitrary")),
        name="merge",
    )(*pres, w_branch, gates, gates, gates, gates)


def _mlp_kernel(*refs, final):
    if final:
        h_ref, g_ref, wu_ref, wd_ref, fn_ref, o_ref, hn_ref = refs
    else:
        h_ref, g_ref, wu_ref, wd_ref, o_ref, hn_ref = refs
    f = pl.program_id(1)

    @pl.when(f == 0)
    def _():
        hn_ref[...] = _rms(h_ref[...], g_ref[...]).astype(BF16)
        o_ref[...] = h_ref[...]

    u = jnp.dot(hn_ref[...], wu_ref[...], preferred_element_type=F32)
    u = jnp.square(jnp.maximum(u, 0.0))
    o_ref[...] += jnp.dot(u.astype(BF16), wd_ref[...], preferred_element_type=F32)

    if final:
        @pl.when(f == pl.num_programs(1) - 1)
        def _():
            o_ref[...] = _rms(o_ref[...], fn_ref[...])


def _mlp(h, norm_g, w_up, w_down, final_g, *, tm, tf):
    R, D = h.shape
    FF = w_up.shape[1]
    final = final_g is not None
    args = [h, norm_g.reshape(1, D), w_up, w_down]
    specs = [pl.BlockSpec((tm, D), lambda i, f: (i, 0)),
             pl.BlockSpec((1, D), lambda i, f: (0, 0)),
             pl.BlockSpec((D, tf), lambda i, f: (0, f)),
             pl.BlockSpec((tf, D), lambda i, f: (f, 0))]
    if final:
        args.append(final_g.reshape(1, D))
        specs.append(pl.BlockSpec((1, D), lambda i, f: (0, 0)))
    return pl.pallas_call(
        functools.partial(_mlp_kernel, final=final),
        out_shape=jax.ShapeDtypeStruct((R, D), F32),
        grid=(R // tm, FF // tf),
        in_specs=specs,
        out_specs=pl.BlockSpec((tm, D), lambda i, f: (i, 0)),
        scratch_shapes=[pltpu.VMEM((tm, D), BF16)],
        compiler_params=_cparams(("parallel", "arbitrary")),
        name="mlp",
    )(*args)


def _conv_kernel(au_ref, ag_ref, bb_ref, bc_ref, bh_ref, dwa_ref, dwab_ref, lng_ref, lnb_ref, cb_ref,
                 pa_ref, pb_ref, sa_ref, sb_ref, xa_ref, xb_ref, *, T, n_pad, wa, wb):
    t = pl.program_id(1)
    ha = 32
    hb = 8

    @pl.when(t > 0)
    def _():
        xa_ref[0:ha, :] = xa_ref[T:T + ha, :]
        xb_ref[0:hb, :] = xb_ref[T:T + hb, :]

    xa_ref[ha:ha + T, :] = au_ref[...] * _sigmoid(ag_ref[...])
    xb_ref[hb:hb + T, :] = bc_ref[...] * bh_ref[...]

    @pl.when(t == 0)
    def _():
        xa_ref[0:ha + n_pad, :] = jnp.zeros((ha + n_pad, xa_ref.shape[1]), F32)
        xb_ref[0:hb + n_pad, :] = jnp.zeros((hb + n_pad, xb_ref.shape[1]), F32)

    def chunk(c, carry):
        r0 = pl.multiple_of(c * SUB, SUB)
        cols = []
        for g in range(xa_ref.shape[1] // LANES):
            ls = slice(g * LANES, (g + 1) * LANES)
            win = xa_ref[pl.ds(r0, ha + SUB), ls]
            a = None
            for rho in range(8):
                taps = [j for j in range(wa) if (ha - (wa - 1) + j) % 8 == rho]
                if not taps:
                    continue
                rot = win if rho == 0 else pltpu.roll(win, ha + SUB - rho, 0)
                for j in taps:
                    o = ha - (wa - 1) + j - rho
                    term = dwa_ref[j:j + 1, ls] * rot[o:o + SUB, :]
                    a = term if a is None else a + term
            cols.append(a)
        acc = jnp.concatenate(cols, axis=1) + dwab_ref[...]
        mu = jnp.mean(acc, axis=-1, keepdims=True)
        cen = acc - mu
        var = jnp.mean(cen * cen, axis=-1, keepdims=True)
        y = cen * lax.rsqrt(var + EPS) * lng_ref[...] + lnb_ref[...]
        pa_ref[pl.ds(r0, SUB), :] = (y * _sigmoid(y)).astype(pa_ref.dtype)
        winb = xb_ref[pl.ds(r0, hb + SUB), :]
        cv = None
        for j in range(wb):
            o = hb - (wb - 1) + j
            term = cb_ref[j:j + 1, :] * winb[o:o + SUB, :]
            cv = term if cv is None else cv + term
        pb_ref[pl.ds(r0, SUB), :] = (bb_ref[pl.ds(r0, SUB), :] * cv).astype(pb_ref.dtype)
        return carry

    lax.fori_loop(0, T // SUB, chunk, 0)

    @pl.when(t == pl.num_programs(1) - 1)
    def _():
        sa_ref[0] = xa_ref[T + ha - (wa - 1):T + ha, :]
        sb_ref[0] = xb_ref[T + hb - (wb - 1):T + hb, :]


def _conv(proj, dw_a, dw_a_bias, ln_g, ln_b, conv_b, *, B, LB, n_pad, W):
    T = SEQ_TILE
    nT = LB // T
    wa, wb = dw_a.shape[0], conv_b.shape[0]
    assert wa - 1 <= 32 and wb - 1 <= 8 and n_pad <= T and n_pad % 8 == 0
    R = proj.shape[0]

    def seg(c):
        return pl.BlockSpec((T, W), lambda b, t: (b * nT + t, c))

    def par(n):
        return pl.BlockSpec((n, W), lambda b, t: (0, 0))

    row = pl.BlockSpec((T, W), lambda b, t: (b * nT + t, 0))
    return pl.pallas_call(
        functools.partial(_conv_kernel, T=T, n_pad=n_pad, wa=wa, wb=wb),
        out_shape=(jax.ShapeDtypeStruct((R, W), BF16), jax.ShapeDtypeStruct((R, W), BF16),
                   jax.ShapeDtypeStruct((B, wa - 1, W), F32), jax.ShapeDtypeStruct((B, wb - 1, W), F32)),
        grid=(B, nT),
        in_specs=[seg(0), seg(1), seg(2), seg(3), seg(4), par(wa), par(1), par(1), par(1), par(wb)],
        out_specs=(row, row,
                   pl.BlockSpec((1, wa - 1, W), lambda b, t: (b, 0, 0)),
                   pl.BlockSpec((1, wb - 1, W), lambda b, t: (b, 0, 0))),
        scratch_shapes=[pltpu.VMEM((T + 32, W), F32), pltpu.VMEM((T + 8, W), F32)],
        compiler_params=_cparams(("parallel", "arbitrary")),
        name="conv",
    )(proj, proj, proj, proj, proj, dw_a, dw_a_bias.reshape(1, W), ln_g.reshape(1, W), ln_b.reshape(1, W), conv_b)


B_DIAG, B_SUB, B_00, B_10, B_X0 = 1, 2, 3, 4, 5


def _rel_bias(qpos, kpos, table, n_buckets):
    n = jnp.maximum(qpos[:, None] - kpos[None, :], 0)
    max_exact = n_buckets // 2
    nf = jnp.maximum(n, 1).astype(F32)
    large = max_exact + (jnp.log(nf / max_exact) / math.log(MAX_DISTANCE / max_exact)
                         * (n_buckets - max_exact)).astype(jnp.int32)
    bucket = jnp.where(n < max_exact, n, jnp.minimum(large, n_buckets - 1))
    onehot = (bucket[:, :, None] == jnp.arange(n_buckets)[None, None, :]).astype(F32)
    bias = jnp.einsum("qkn,nh->hqk", onehot, table.astype(F32), precision=lax.Precision.HIGHEST)
    return bias - table[n_buckets - 1].astype(F32)[:, None, None]


def _bias_bank(table, T, n_pad):
    nb, H = table.shape
    r = jnp.arange(T)
    causal = (r[None, :] <= r[:, None])[None]
    kpad = (r < n_pad)[None, None, :]
    qval = (r >= n_pad)[None, :, None]
    zero = jnp.zeros((H, T, T), F32)
    diag = jnp.where(causal, _rel_bias(r, r, table, nb), NEG)
    sub = _rel_bias(r + T, r, table, nb)
    b00 = jnp.where(kpad & qval, NEG, diag)
    b10 = jnp.where(kpad, NEG, sub)
    bx0 = jnp.where(kpad, NEG, zero)
    return jnp.stack([zero, diag, sub, b00, b10, bx0], axis=0) * LOG2E


def _sublane_all(x, op):
    for sh in (4, 2, 1):
        x = op(x, pltpu.roll(x, sh, 0))
    return x


def _attn_kernel(lam_ref, q_ref, k_ref, v_ref, bias_ref, sub_ref, o_ref, qm_ref, k16_ref, vt_ref, m_ref, l_ref,
                 acc_ref, *, T, LB, dk, dv, scale, out_scale):
    i = pl.program_id(2)
    S8 = 8

    @pl.when(i == 0)
    def _():
        k16_ref[...] = k_ref[...].astype(BF16)
        for c in range(LB // LANES):
            r0 = c * LANES
            vt_ref[r0 // T, :, r0 % T:r0 % T + LANES] = v_ref[r0:r0 + LANES, :].T.astype(BF16)

    q = q_ref[...] * (scale * LOG2E)
    lane = lax.broadcasted_iota(jnp.int32, q.shape, 1)
    qm_ref[0:T, :] = jnp.where(lane < dk, q, 0.0).astype(BF16)
    qm_ref[T:2 * T, :] = jnp.where(lane >= dk, q, 0.0).astype(BF16)

    m_ref[...] = jnp.full(m_ref.shape, NEG, F32)
    l_ref[...] = jnp.zeros(l_ref.shape, F32)
    acc_ref[...] = jnp.zeros(acc_ref.shape, F32)

    def tile(j, bidx, nt=1):
        r0 = pl.multiple_of(j * T, T)
        k = k16_ref[pl.ds(r0, nt * T), :]
        s = lax.dot_general(k, qm_ref[...], (((1,), (1,)), ((), ())), preferred_element_type=F32)
        if bidx is not None:
            bias = bias_ref[bidx, 0]
            s = jnp.concatenate([s[:, 0:T] + bias, s[:, T:2 * T] + bias], axis=1)
        s3 = s.reshape(nt * T // S8, S8, 2 * T)
        m_old = m_ref[...]
        m_new = jnp.maximum(m_old, _sublane_all(jnp.max(s3, axis=0), jnp.maximum))
        alpha = jnp.exp2(m_old - m_new)
        p3 = jnp.exp2(s3 - m_new[None])
        l_ref[...] = alpha * l_ref[...] + jnp.sum(p3, axis=0)
        p = p3.reshape(nt * T, 2 * T).astype(BF16)
        pv = jnp.dot(vt_ref[j], p[0:T], preferred_element_type=F32)
        for u in range(1, nt):
            pv = pv + jnp.dot(vt_ref[j + u], p[u * T:(u + 1) * T], preferred_element_type=F32)
        acc_ref[...] = alpha[None] * acc_ref[...] + pv.reshape(dv // S8, S8, 2 * T)
        m_ref[...] = m_new

    tile(0, jnp.where(i == 0, B_00, jnp.where(i == 1, B_10, B_X0)))

    n_far = jnp.maximum(i - 2, 0)

    def far(jj, carry):
        tile(1 + 2 * jj, None, nt=2)
        return carry

    lax.fori_loop(0, n_far // 2, far, 0)

    @pl.when(n_far % 2 == 1)
    def _():
        tile(i - 2, None)

    @pl.when(i >= 2)
    def _():
        tile(i - 1, B_SUB)

    @pl.when(i >= 1)
    def _():
        tile(i, B_DIAG)

    lam = lam_ref[0]
    on = acc_ref[...] / _sublane_all(l_ref[...], jnp.add)[None]
    o3 = on[:, :, 0:T] - lam * on[:, :, T:2 * T]
    ms = _sublane_all(jnp.sum(o3 * o3, axis=0), jnp.add) * (1.0 / dv)
    y = (o3 * lax.rsqrt(ms + EPS)[None]).reshape(dv, T).T
    o_ref[...] = (y * sub_ref[...] * out_scale).astype(o_ref.dtype)


def _attn(proj, bank, lam, subln, *, B, LB, H, dv, lam_init):
    T = SEQ_TILE
    nQ = LB // T
    R = proj.shape[0]
    dk = dv // 2
    qc, kc, vc = 5 * H, 6 * H, 7 * H
    return pl.pallas_call(
        functools.partial(_attn_kernel, T=T, LB=LB, dk=dk, dv=dv, scale=dk ** -0.5, out_scale=1.0 - lam_init),
        out_shape=jax.ShapeDtypeStruct((R, H * dv), BF16),
        grid=(B, H, nQ),
        in_specs=[pl.BlockSpec(memory_space=pltpu.SMEM),
                  pl.BlockSpec((T, dv), lambda b, h, i: (b * nQ + i, qc + h)),
                  pl.BlockSpec((LB, dv), lambda b, h, i: (b, kc + h)),
                  pl.BlockSpec((LB, dv), lambda b, h, i: (b, vc + h)),
                  pl.BlockSpec((6, 1, T, T), lambda b, h, i: (0, h, 0, 0)),
                  pl.BlockSpec((1, dv), lambda b, h, i: (0, 0))],
        out_specs=pl.BlockSpec((T, dv), lambda b, h, i: (b * nQ + i, h)),
        scratch_shapes=[pltpu.VMEM((2 * T, dv), BF16), pltpu.VMEM((LB, dv), BF16), pltpu.VMEM((nQ, dv, T), BF16),
                        pltpu.VMEM((8, 2 * T), F32), pltpu.VMEM((8, 2 * T), F32),
                        pltpu.VMEM((dv // 8, 8, 2 * T), F32)],
        compiler_params=_cparams(("parallel", "parallel", "arbitrary")),
        name="attn",
    )(lam, proj, proj, proj, jnp.swapaxes(bank, -1, -2), subln.reshape(1, dv))


def _log_forget(zf, log_lb, log1m_lb):
    ls = jnp.minimum(zf, 0.0) - jnp.log1p(jnp.exp(-jnp.abs(zf)))
    b = log1m_lb + ls
    hi = jnp.maximum(log_lb, b)
    lo = jnp.minimum(log_lb, b)
    return hi + jnp.log1p(jnp.exp(lo - hi))


def _split3(x):
    hi = x.astype(BF16)
    r = x - hi.astype(F32)
    mid = r.astype(BF16)
    lo = (r - mid.astype(F32)).astype(BF16)
    return hi, mid, lo


def _hgrn_kernel(q_ref, f_ref, i_ref, g_ref, loglb_ref, log1m_ref, omlb_ref, gn_ref,
                 o_ref, s_ref, st_ref, *, T, n_pad, H, dk):
    t = pl.program_id(1)

    @pl.when(t == 0)
    def _():
        st_ref[...] = jnp.zeros(st_ref.shape, F32)

    U = 8
    rr = lax.broadcasted_iota(jnp.int32, (SUB, SUB), 0)
    cc = lax.broadcasted_iota(jnp.int32, (SUB, SUB), 1)
    tri = jnp.where(cc <= rr, 1.0, 0.0).astype(BF16)
    row = lax.broadcasted_iota(jnp.int32, (SUB, 1), 0)
    urow = lax.broadcasted_iota(jnp.int32, (U, 1), 0)

    def chunk(c, carry):
        r0 = pl.multiple_of(c * SUB, SUB)
        zf = f_ref[pl.ds(r0, SUB), :]
        logf = _log_forget(zf, loglb_ref[...], log1m_ref[...])
        hi, mid, lo = _split3(logf)
        bcum = (jnp.dot(tri, hi, preferred_element_type=F32) + jnp.dot(tri, mid, preferred_element_type=F32)
                + jnp.dot(tri, lo, preferred_element_type=F32))
        kd = omlb_ref[...] * _sigmoid(-zf)
        zq = q_ref[pl.ds(r0, SUB), :]
        qd = zq * _sigmoid(zq)
        seq_row = t * T + r0 + row
        vd = jnp.where(seq_row >= n_pad, i_ref[pl.ds(r0, SUB), :], 0.0)
        zg = g_ref[pl.ds(r0, SUB), :]
        og = zg * _sigmoid(zg)
        blast, bmid = bcum[SUB - 1:SUB, :], bcum[U - 1:U, :]
        qs = (qd * jnp.exp(bcum)).astype(BF16)
        kt = (kd * jnp.exp(blast - bcum)).astype(BF16)
        q1 = (qd[U:SUB] * jnp.exp(bcum[U:SUB] - bmid)).astype(BF16)
        k0 = (kd[0:U] * jnp.exp(bmid - bcum[0:U])).astype(BF16)
        v16 = vd.astype(BF16)
        glast = jnp.exp(blast)
        o_state, a10 = [], []
        for h in range(H):
            sl = slice(h * dk, (h + 1) * dk)
            st = st_ref[h]
            o_state.append(lax.dot_general(qs[:, sl], st.astype(BF16), (((1,), (1,)), ((), ())),
                                           preferred_element_type=F32))
            a10.append(lax.dot_general(q1[:, sl], k0[:, sl], (((1,), (1,)), ((), ())),
                                       preferred_element_type=F32))
            kv = lax.dot_general(v16[:, sl], kt[:, sl], (((0,), (0,)), ((), ())), preferred_element_type=F32)
            st_ref[h] = st * glast[:, sl] + kv
        intra = []
        for h in range(H):
            sl = slice(h * dk, (h + 1) * dk)
            units = []
            for u in range(SUB // U):
                us = slice(u * U, (u + 1) * U)
                bu, qu, ku, vu = bcum[us, sl], qd[us, sl], kd[us, sl], vd[us, sl]
                a = None
                for d in range(U):
                    if d == 0:
                        kr, br, vr = ku, bu, vu
                    else:
                        kr, br, vr = pltpu.roll(ku, d, 0), pltpu.roll(bu, d, 0), pltpu.roll(vu, d, 0)
                    w = jnp.where(urow >= d, qu * kr * jnp.exp(bu - br), 0.0)
                    term = jnp.sum(w, axis=-1, keepdims=True) * vr
                    a = term if a is None else a + term
                units.append(a)
            intra.append(jnp.concatenate(units, axis=0))
        for h in range(H):
            sl = slice(h * dk, (h + 1) * dk)
            cross = jnp.dot(a10[h].astype(BF16), v16[0:U, sl], preferred_element_type=F32)
            o = o_state[h] + intra[h] + jnp.concatenate([jnp.zeros((U, dk), F32), cross], axis=0)
            o_ref[pl.ds(r0, SUB), sl] = (_rms(o, gn_ref[...]) * og[:, sl]).astype(o_ref.dtype)
        return carry

    lax.fori_loop(0, T // SUB, chunk, 0)

    @pl.when(t == pl.num_programs(1) - 1)
    def _():
        for h in range(H):
            s_ref[0, h] = st_ref[h].T


def _hgrn(proj, log_lb, log1m_lb, om_lb, g_norm, *, B, LB, n_pad, H, dk):
    T = SEQ_TILE
    nT = LB // T
    R = proj.shape[0]
    W = H * dk

    def seg(c):
        return pl.BlockSpec((T, W), lambda b, t: (b * nT + t, c))

    def par(n):
        return pl.BlockSpec((1, n), lambda b, t: (0, 0))

    return pl.pallas_call(
        functools.partial(_hgrn_kernel, T=T, n_pad=n_pad, H=H, dk=dk),
        out_shape=(jax.ShapeDtypeStruct((R, W), BF16), jax.ShapeDtypeStruct((B, H, dk, dk), F32)),
        grid=(B, nT),
        in_specs=[seg(8), seg(9), seg(10), seg(11), par(W), par(W), par(W), par(dk)],
        out_specs=(pl.BlockSpec((T, W), lambda b, t: (b * nT + t, 0)),
                   pl.BlockSpec((1, H, dk, dk), lambda b, t: (b, 0, 0, 0))),
        scratch_shapes=[pltpu.VMEM((H, dk, dk), F32)],
        compiler_params=_cparams(("parallel", "arbitrary")),
        name="hgrn",
    )(proj, proj, proj, proj, log_lb, log1m_lb, om_lb, g_norm.reshape(1, dk))


def _sample_mix_kernel(p_ref, sa_ref, sb_ref, sh_ref, dwa_ref, dwab_ref, lng_ref, lnb_ref, cb_ref,
                       lb_ref, omlb_ref, gn_ref, pa_in, pb_in, pd_in,
                       pa_ref, pb_ref, pd_ref, nsa_ref, nsb_ref, nsh_ref, o_scr, *, NB, W, H, dk, wa, wb):
    del pa_in, pb_in, pd_in

    def seg(c):
        return p_ref[:, c * W:(c + 1) * W]

    pad = jnp.zeros((SUB - NB, W), F32)

    glu = seg(0) * _sigmoid(seg(1))
    rows = []
    for b in range(NB):
        cv = jnp.sum(sa_ref[b] * dwa_ref[0:wa - 1, :], axis=0, keepdims=True)
        rows.append(cv + dwa_ref[wa - 1:wa, :] * glu[b:b + 1, :])
        nsa_ref[b, 0:wa - 2, :] = sa_ref[b, 1:wa - 1, :]
        nsa_ref[b, wa - 2:wa - 1, :] = glu[b:b + 1, :]
    acc = jnp.concatenate(rows, axis=0) + dwab_ref[...]
    mu = jnp.mean(acc, axis=-1, keepdims=True)
    cen = acc - mu
    var = jnp.mean(cen * cen, axis=-1, keepdims=True)
    y = cen * lax.rsqrt(var + EPS) * lng_ref[...] + lnb_ref[...]
    pa_ref[...] = jnp.concatenate([y * _sigmoid(y), pad], axis=0).astype(pa_ref.dtype)

    u = seg(3) * seg(4)
    rows = []
    for b in range(NB):
        cv = jnp.sum(sb_ref[b] * cb_ref[0:wb - 1, :], axis=0, keepdims=True)
        rows.append(cv + cb_ref[wb - 1:wb, :] * u[b:b + 1, :])
        if wb > 2:
            nsb_ref[b, 0:wb - 2, :] = sb_ref[b, 1:wb - 1, :]
        nsb_ref[b, wb - 2:wb - 1, :] = u[b:b + 1, :]
    pb_ref[...] = jnp.concatenate([seg(2) * jnp.concatenate(rows, axis=0), pad], axis=0).astype(pb_ref.dtype)

    zf = seg(9)
    sg = _sigmoid(zf)
    fg = lb_ref[...] + omlb_ref[...] * sg
    kd = omlb_ref[...] * _sigmoid(-zf)
    zq = seg(8)
    qd = zq * _sigmoid(zq)
    vd = seg(10)
    eye = lax.broadcasted_iota(jnp.int32, (dk, dk), 0) == lax.broadcasted_iota(jnp.int32, (dk, dk), 1)

    def col(x):
        return jnp.sum(jnp.where(eye, x, 0.0), axis=1, keepdims=True)

    for b in range(NB):
        for h in range(H):
            sl = slice(h * dk, (h + 1) * dk)
            s_new = col(fg[b:b + 1, sl]) * sh_ref[b, h] + col(kd[b:b + 1, sl]) * vd[b:b + 1, sl]
            nsh_ref[b, h] = s_new
            o_scr[b:b + 1, sl] = jnp.sum(col(qd[b:b + 1, sl]) * s_new, axis=0, keepdims=True)
    zg = seg(11)
    og = zg * _sigmoid(zg)
    o = o_scr[...]
    outs = [_rms(o[:, h * dk:(h + 1) * dk], gn_ref[...]) for h in range(H)]
    pd_ref[...] = jnp.concatenate([jnp.concatenate(outs, axis=1) * og, pad], axis=0).astype(pd_ref.dtype)


def _sample_mix(proj, sa, sb, sh, dw_a, dw_a_bias, ln_g, ln_b, conv_b, lb, om_lb, g_norm, pre_a, pre_b, pre_d):
    NB, wa1, W = sa.shape
    wb1 = sb.shape[1]
    H, dk = sh.shape[1], sh.shape[2]
    R = proj.shape[0]

    def full(shape):
        return pl.BlockSpec(shape, lambda i: (0,) * len(shape))

    rows = pl.BlockSpec((SUB, W), lambda i: (0, 0))
    return pl.pallas_call(
        functools.partial(_sample_mix_kernel, NB=NB, W=W, H=H, dk=dk, wa=wa1 + 1, wb=wb1 + 1),
        out_shape=(jax.ShapeDtypeStruct((R, W), BF16), jax.ShapeDtypeStruct((R, W), BF16),
                   jax.ShapeDtypeStruct((R, W), BF16), jax.ShapeDtypeStruct(sa.shape, F32),
                   jax.ShapeDtypeStruct(sb.shape, F32), jax.ShapeDtypeStruct(sh.shape, F32)),
        grid=(1,),
        in_specs=[pl.BlockSpec((NB, N_SEG * W), lambda i: (0, 0)), full(sa.shape), full(sb.shape), full(sh.shape),
                  full((wa1 + 1, W)), full((1, W)), full((1, W)), full((1, W)), full((wb1 + 1, W)),
                  full((1, W)), full((1, W)), full((1, dk)),
                  pl.BlockSpec(memory_space=pl.ANY), pl.BlockSpec(memory_space=pl.ANY),
                  pl.BlockSpec(memory_space=pl.ANY)],
        out_specs=(rows, rows, rows, full(sa.shape), full(sb.shape), full(sh.shape)),
        scratch_shapes=[pltpu.VMEM((NB, W), F32)],
        input_output_aliases={12: 0, 13: 1, 14: 2},
        compiler_params=_cparams(("arbitrary",)),
        name="sample_mix",
    )(proj, sa, sb, sh, dw_a, dw_a_bias.reshape(1, W), ln_g.reshape(1, W), ln_b.reshape(1, W), conv_b,
      lb, om_lb, g_norm.reshape(1, dk), pre_a, pre_b, pre_d)


def _decode_kernel(pt_ref, lam_ref, q_ref, kn_ref, vn_ref, bias_ref, biasn_ref, sub_ref, pc_in, *rest,
                   G, P, H, dv, scale, out_scale, NB):
    del pt_ref, pc_in
    k_refs, v_refs = rest[:G], rest[G:2 * G]
    o_ref, m_ref, l_ref, acc_ref, o_scr = rest[2 * G:]
    b = pl.program_id(0)
    s = pl.program_id(1)
    dk = dv // 2
    HM = 2 * H
    W = H * dv
    C = P * H
    last = s == pl.num_programs(1) - 1

    @pl.when(s == 0)
    def _():
        m_ref[...] = jnp.full(m_ref.shape, NEG, F32)
        l_ref[...] = jnp.zeros(l_ref.shape, F32)
        acc_ref[...] = jnp.zeros(acc_ref.shape, F32)

    def per_map_rows(row):
        return jnp.concatenate([row[:, h * dv:(h + 1) * dv] for h in range(H) for _ in range(2)], axis=0)

    rr = lax.broadcasted_iota(jnp.int32, (HM, dv), 0)
    ll = lax.broadcasted_iota(jnp.int32, (HM, dv), 1)
    own_map = (ll >= dk) == (jnp.bitwise_and(rr, 1) == 1)
    qall = jnp.where(own_map, per_map_rows(q_ref[pl.ds(b, 1), :]) * (scale * LOG2E), 0.0)
    qall16 = qall.astype(BF16)
    col_head = jnp.bitwise_and(lax.broadcasted_iota(jnp.int32, (HM, C), 1), H - 1)
    row_head = lax.shift_right_logical(lax.broadcasted_iota(jnp.int32, (HM, C), 0), 1)
    own_head = col_head == row_head

    blocks = []
    for g in range(G):
        kg = k_refs[g][...].reshape(C, dv).astype(BF16)
        sg = lax.dot_general(qall16, kg, (((1,), (1,)), ((), ())), preferred_element_type=F32)
        if g == G - 1:
            sg = sg + jnp.where(last, bias_ref[...], 0.0)
        blocks.append(jnp.where(own_head, sg, NEG))
    sc = jnp.concatenate(blocks, axis=1)
    m_old = m_ref[...]
    m_new = jnp.maximum(m_old, jnp.max(sc, axis=-1, keepdims=True))
    alpha = jnp.exp2(m_old - m_new)
    p = jnp.exp2(sc - m_new)
    l_ref[...] = alpha * l_ref[...] + jnp.sum(p, axis=-1, keepdims=True)
    pv = None
    for g in range(G):
        vg = v_refs[g][...].reshape(C, dv).astype(BF16)
        y = jnp.dot(p[:, g * C:(g + 1) * C].astype(BF16), vg, preferred_element_type=F32)
        pv = y if pv is None else pv + y
    acc_ref[...] = alpha * acc_ref[...] + pv
    m_ref[...] = m_new

    @pl.when(last)
    def _():
        kn = per_map_rows(kn_ref[pl.ds(b, 1), :])
        vn = per_map_rows(vn_ref[pl.ds(b, 1), :])
        sn = jnp.sum(qall * kn, axis=-1, keepdims=True) + biasn_ref[...]
        m1 = m_ref[...]
        m2 = jnp.maximum(m1, sn)
        a2 = jnp.exp2(m1 - m2)
        pn = jnp.exp2(sn - m2)
        o = (a2 * acc_ref[...] + pn * vn) / (a2 * l_ref[...] + pn)
        lam = lam_ref[0]
        outs = []
        for h in range(H):
            oh = o[2 * h:2 * h + 1, :] - lam * o[2 * h + 1:2 * h + 2, :]
            outs.append(_rms(oh, sub_ref[...]) * out_scale)
        o_scr[pl.ds(b, 1), :] = jnp.concatenate(outs, axis=1)

    @pl.when((s == pl.num_programs(1) - 1) & (b == NB - 1))
    def _():
        o_ref[...] = jnp.concatenate([o_scr[...], jnp.zeros((SUB - NB, W), F32)], axis=0).astype(o_ref.dtype)


def _decode_attn(proj, cache_k, cache_v, page_table, layer, bias_past, bias_new, lam, subln, pre_c,
                 *, H, dv, lam_init):
    NB, n_pages = page_table.shape
    P = cache_k.shape[2]
    G = PAGES_PER_STEP
    assert n_pages % G == 0 and P >= MAX_DISTANCE and H & (H - 1) == 0
    W = H * dv
    HM = 2 * H

    def page(g):
        return pl.BlockSpec((None, None, P, H, dv), lambda b, s, pt: (pt[b, s * G + g], layer, 0, 0, 0))

    def rows(c):
        return pl.BlockSpec((NB, W), lambda b, s, pt: (0, c))

    grid_spec = pltpu.PrefetchScalarGridSpec(
        num_scalar_prefetch=1,
        grid=(NB, n_pages // G),
        in_specs=[pl.BlockSpec(memory_space=pltpu.SMEM), rows(5), rows(6), rows(7),
                  pl.BlockSpec((HM, P * H), lambda b, s, pt: (0, 0)),
                  pl.BlockSpec((HM, 1), lambda b, s, pt: (0, 0)),
                  pl.BlockSpec((1, dv), lambda b, s, pt: (0, 0)),
                  pl.BlockSpec(memory_space=pl.ANY)]
        + [page(g) for g in range(G)] + [page(g) for g in range(G)],
        out_specs=pl.BlockSpec((SUB, W), lambda b, s, pt: (0, 0)),
        scratch_shapes=[pltpu.VMEM((HM, 1), F32), pltpu.VMEM((HM, 1), F32), pltpu.VMEM((HM, dv), F32),
                        pltpu.VMEM((NB, W), F32)],
    )
    return pl.pallas_call(
        functools.partial(_decode_kernel, G=G, P=P, H=H, dv=dv, scale=(dv // 2) ** -0.5,
                          out_scale=1.0 - lam_init, NB=NB),
        out_shape=jax.ShapeDtypeStruct(pre_c.shape, pre_c.dtype),
        grid_spec=grid_spec,
        input_output_aliases={8: 0},
        compiler_params=_cparams(("arbitrary", "arbitrary")),
        name="decode_attn",
    )(page_table, lam, proj, proj, proj, bias_past, bias_new, subln.reshape(1, dv), pre_c,
      *([cache_k] * G), *([cache_v] * G))


def kernel(x_prompt, x_sample, cache_k, cache_v, state_conv_a, state_conv_b, state_hgrn, page_table, meta_tokens, rel_bias_table, hgrn_lower_bound, norm1, w_in, dw_a, dw_a_bias, ln_a_g, ln_a_b, w_a_out, conv_b, w_b_out, lam_q1, lam_k1, lam_q2, lam_k2, subln, w_c_out, g_norm_d, w_d_out, w_gate, b_gate, w_o, norm2, w_up, w_down, final_norm):
    B, L, D = x_prompt.shape
    NB = x_sample.shape[0]
    assert x_sample.shape[1] == 1
    depth = w_in.shape[0]
    n_meta = meta_tokens.shape[0]
    W = state_conv_a.shape[-1]
    H, dv = cache_v.shape[3], cache_v.shape[4]
    Hr, dkr = state_hgrn.shape[2], state_hgrn.shape[3]
    n_buckets = rel_bias_table.shape[0]
    past_len = page_table.shape[1] * cache_k.shape[2]
    assert w_in.shape[2] == N_SEG * W and H * dv == W and Hr * dkr == W and state_conv_b.shape[-1] == W
    assert cache_k.shape[4] == dv and state_hgrn.shape[4] == dkr and dv == LANES and dkr == LANES

    T = SEQ_TILE
    Lt = n_meta + L
    LB = -(-(Lt + NB) // T) * T
    n_pad = LB - Lt
    R = B * LB
    tm = ROW_TILE if R % ROW_TILE == 0 else T
    assert n_pad <= T and NB <= SUB <= n_pad and R % tm == 0

    meta = jnp.broadcast_to(meta_tokens[None].astype(F32), (B, n_meta, D))
    h = jnp.concatenate([jnp.zeros((B, n_pad, D), F32), meta, x_prompt], axis=1)
    h = h.at[0, :NB].set(x_sample[:, 0]).reshape(R, D)

    lbs = jnp.cumsum(jax.nn.softmax(hgrn_lower_bound.astype(F32), axis=0), axis=0)
    lbs = lbs - lbs[0:1]
    bank = _bias_bank(rel_bias_table, T, n_pad)
    pos_s = jnp.full((1,), past_len, jnp.int32)
    P = cache_k.shape[2]
    kpos_near = jnp.arange(past_len - P, past_len + 1)
    bias_dec = jnp.repeat(_rel_bias(pos_s, kpos_near, rel_bias_table, n_buckets)[:, 0, :], 2, axis=0) * LOG2E
    bias_past, bias_new = jnp.repeat(bias_dec[:, :P], H, axis=1), bias_dec[:, P:]

    k_p, v_p, ca_p, cb_p, s_p = [], [], [], [], []
    k_s, v_s, ca_s, cb_s, s_s = [], [], [], [], []
    y = None
    for l in range(depth):
        lam_init = 0.8 - 0.6 * math.exp(-0.3 * l)
        lam = (jnp.exp(jnp.sum(lam_q1[l].astype(F32) * lam_k1[l].astype(F32)))
               - jnp.exp(jnp.sum(lam_q2[l].astype(F32) * lam_k2[l].astype(F32))) + lam_init).reshape(1)
        lb = lbs[l].reshape(1, W)
        log_lb, log1m_lb, om_lb = jnp.log(lb), jnp.log1p(-lb), 1.0 - lb
        w_branch = jnp.stack([w_a_out[l], w_b_out[l], w_c_out[l], w_d_out[l]], axis=0).astype(BF16)

        proj = _matmul(h, w_in[l].astype(BF16), tm=tm, tn=COL_TILE, out_dtype=F32, norm_g=norm1[l], name="proj")
        gates = _matmul(h, w_gate[l].astype(BF16), tm=tm, tn=COL_TILE, out_dtype=BF16, norm_g=norm1[l],
                        bias=b_gate[l], act="sigmoid", name="gates")

        pre_a, pre_b, ca, cb = _conv(proj, dw_a[l], dw_a_bias[l], ln_a_g[l], ln_a_b[l], conv_b[l],
                                     B=B, LB=LB, n_pad=n_pad, W=W)
        pre_c = _attn(proj, bank, lam, subln[l], B=B, LB=LB, H=H, dv=dv, lam_init=lam_init)
        pre_d, s_fin = _hgrn(proj, log_lb, log1m_lb, om_lb, g_norm_d[l], B=B, LB=LB, n_pad=n_pad, H=Hr, dk=dkr)

        pre_a, pre_b, pre_d, nsa, nsb, nsh = _sample_mix(
            proj, state_conv_a[l], state_conv_b[l], state_hgrn[l], dw_a[l], dw_a_bias[l], ln_a_g[l], ln_a_b[l],
            conv_b[l], lb, om_lb, g_norm_d[l], pre_a, pre_b, pre_d)
        pre_c = _decode_attn(proj, cache_k, cache_v, page_table, l, bias_past, bias_new, lam, subln[l], pre_c,
                             H=H, dv=dv, lam_init=lam_init)

        merged = _merge((pre_a, pre_b, pre_c, pre_d), w_branch, gates, tm=tm, tn=MERGE_COL_TILE)
        h = _matmul(merged, w_o[l].astype(BF16), tm=tm, tn=COL_TILE, out_dtype=F32, residual=h, name="w_o")
        last = l == depth - 1
        h = _mlp(h, norm2[l], w_up[l].astype(BF16), w_down[l].astype(BF16), final_norm if last else None,
                 tm=tm, tf=FF_TILE)

        proj3 = proj.reshape(B, LB, N_SEG * W)
        k_p.append(proj3[:, n_pad:, 6 * W:7 * W])
        v_p.append(proj3[:, n_pad:, 7 * W:8 * W])
        k_s.append(proj[:NB, 6 * W:7 * W])
        v_s.append(proj[:NB, 7 * W:8 * W])
        ca_p.append(ca); cb_p.append(cb); s_p.append(s_fin)
        ca_s.append(nsa); cb_s.append(nsb); s_s.append(nsh)

    y3 = h.reshape(B, LB, D)
    y_prompt = y3[:, n_pad + n_meta:]
    y_sample = y3[0, :NB].reshape(NB, 1, D)
    new_k_prompt = jnp.stack(k_p, axis=2).reshape(B, Lt, depth, H, dv)
    new_v_prompt = jnp.stack(v_p, axis=2).reshape(B, Lt, depth, H, dv)
    new_k_sample = jnp.stack(k_s, axis=1).reshape(NB, 1, depth, H, dv)
    new_v_sample = jnp.stack(v_s, axis=1).reshape(NB, 1, depth, H, dv)
    return (y_prompt, y_sample, new_k_prompt, new_v_prompt, jnp.stack(ca_p, 0), jnp.stack(cb_p, 0),
            jnp.stack(s_p, 0), new_k_sample, new_v_sample, jnp.stack(ca_s, 0), jnp.stack(cb_s, 0),
            jnp.stack(s_s, 0))
```

```python
import functools
import math

import jax
import jax.numpy as jnp
from jax import lax
from jax.experimental import pallas as pl
from jax.experimental.pallas import tpu as pltpu

F32 = jnp.float32
BF16 = jnp.bfloat16

EPS = 1e-6
MAX_DISTANCE = 128
NEG = -1e30
LOG2E = math.log2(math.e)

LANES = 128
SEQ_TILE = 384
ROW_TILE = 768
COL_TILE = 1024
MERGE_COL_TILE = 512
FF_TILE = 1024
SUB = 16
PAGES_PER_STEP = 8
VMEM_LIMIT = 56 * 1024 * 1024
N_SEG = 12
N_BRANCH = 4


def _cparams(sem):
    return pltpu.CompilerParams(dimension_semantics=sem, vmem_limit_bytes=VMEM_LIMIT)


def _sigmoid(x):
    return 1.0 / (1.0 + jnp.exp(-x))


def _rms(x, g):
    return x * lax.rsqrt(jnp.mean(x * x, axis=-1, keepdims=True) + EPS) * g


def _mm_kernel(*refs, has_norm, has_bias, act, has_res):
    it = iter(refs)
    x_ref = next(it)
    g_ref = next(it) if has_norm else None
    w_ref = next(it)
    b_ref = next(it) if has_bias else None
    r_ref = next(it) if has_res else None
    o_ref = next(it)
    if has_norm:
        xn_ref = next(it)

        @pl.when(pl.program_id(1) == 0)
        def _():
            xn_ref[...] = _rms(x_ref[...], g_ref[...]).astype(BF16)

        lhs = xn_ref[...]
    else:
        lhs = x_ref[...]
    y = jnp.dot(lhs, w_ref[...], preferred_element_type=F32)
    if has_bias:
        y = y + b_ref[...]
    if act == "sigmoid":
        y = _sigmoid(y)
    if has_res:
        y = y + r_ref[...]
    o_ref[...] = y.astype(o_ref.dtype)


def _matmul(x, w, layer, *, tm, tn, out_dtype, norm_g=None, bias=None, act=None, residual=None, name):
    M, K = x.shape
    N = w.shape[2]
    tn = min(tn, N)
    assert M % tm == 0 and N % tn == 0
    args = [x]
    specs = [pl.BlockSpec((tm, K), lambda i, j: (i, 0))]
    scratch = []
    if norm_g is not None:
        args.append(norm_g.reshape(1, K))
        specs.append(pl.BlockSpec((1, K), lambda i, j: (0, 0)))
        scratch.append(pltpu.VMEM((tm, K), BF16))
    args.append(w)
    specs.append(pl.BlockSpec((None, K, tn), lambda i, j: (layer, 0, j)))
    if bias is not None:
        args.append(bias.reshape(1, N))
        specs.append(pl.BlockSpec((1, tn), lambda i, j: (0, j)))
    if residual is not None:
        args.append(residual)
        specs.append(pl.BlockSpec((tm, tn), lambda i, j: (i, j)))
    kern = functools.partial(_mm_kernel, has_norm=norm_g is not None, has_bias=bias is not None,
                             act=act, has_res=residual is not None)
    return pl.pallas_call(
        kern,
        out_shape=jax.ShapeDtypeStruct((M, N), out_dtype),
        grid=(M // tm, N // tn),
        in_specs=specs,
        out_specs=pl.BlockSpec((tm, tn), lambda i, j: (i, j)),
        scratch_shapes=scratch,
        compiler_params=_cparams(("parallel", "arbitrary")),
        name=name,
    )(*args)


def _merge_kernel(pa_ref, pb_ref, pc_ref, pd_ref, w_ref, g0_ref, g1_ref, g2_ref, g3_ref, o_ref):
    acc = None
    for k, (p_ref, g_ref) in enumerate(((pa_ref, g0_ref), (pb_ref, g1_ref), (pc_ref, g2_ref), (pd_ref, g3_ref))):
        y = jnp.dot(p_ref[...], w_ref[k], preferred_element_type=F32) * g_ref[...].astype(F32)
        acc = y if acc is None else acc + y
    o_ref[...] = acc.astype(o_ref.dtype)


def _merge(pres, w_branch, layer, gates, *, tm, tn):
    R, W = pres[0].shape
    D = w_branch.shape[3]
    tn = min(tn, D)
    n_col = D // tn
    pre_spec = pl.BlockSpec((tm, W), lambda i, j: (i, 0))
    gate_specs = [pl.BlockSpec((tm, tn), functools.partial(lambda i, j, b: (i, b * n_col + j), b=b))
                  for b in range(N_BRANCH)]
    return pl.pallas_call(
        _merge_kernel,
        out_shape=jax.ShapeDtypeStruct((R, D), BF16),
        grid=(R // tm, n_col),
        in_specs=([pre_spec] * N_BRANCH
                  + [pl.BlockSpec((None, N_BRANCH, W, tn), lambda i, j: (layer, 0, 0, j))] + gate_specs),
        out_specs=pl.BlockSpec((tm, tn), lambda i, j: (i, j)),
        compiler_params=_cparams(("parallel", "arbitrary")),
        name="merge",
    )(*pres, w_branch, gates, gates, gates, gates)


def _mlp_kernel(*refs, final):
    if final:
        h_ref, g_ref, wu_ref, wd_ref, fn_ref, o_ref, hn_ref = refs
    else:
        h_ref, g_ref, wu_ref, wd_ref, o_ref, hn_ref = refs
    f = pl.program_id(1)

    @pl.when(f == 0)
    def _():
        hn_ref[...] = _rms(h_ref[...], g_ref[...]).astype(BF16)
        o_ref[...] = h_ref[...]

    u = jnp.dot(hn_ref[...], wu_ref[...], preferred_element_type=F32)
    u = jnp.square(jnp.maximum(u, 0.0))
    o_ref[...] += jnp.dot(u.astype(BF16), wd_ref[...], preferred_element_type=F32)

    if final:
        @pl.when(f == pl.num_programs(1) - 1)
        def _():
            o_ref[...] = _rms(o_ref[...], fn_ref[...])


def _mlp(h, norm_g, w_up, w_down, layer, final_g, *, tm, tf):
    R, D = h.shape
    FF = w_up.shape[2]
    final = final_g is not None
    args = [h, norm_g.reshape(1, D), w_up, w_down]
    specs = [pl.BlockSpec((tm, D), lambda i, f: (i, 0)),
             pl.BlockSpec((1, D), lambda i, f: (0, 0)),
             pl.BlockSpec((None, D, tf), lambda i, f: (layer, 0, f)),
             pl.BlockSpec((None, tf, D), lambda i, f: (layer, f, 0))]
    if final:
        args.append(final_g.reshape(1, D))
        specs.append(pl.BlockSpec((1, D), lambda i, f: (0, 0)))
    return pl.pallas_call(
        functools.partial(_mlp_kernel, final=final),
        out_shape=jax.ShapeDtypeStruct((R, D), F32),
        grid=(R // tm, FF // tf),
        in_specs=specs,
        out_specs=pl.BlockSpec((tm, D), lambda i, f: (i, 0)),
        scratch_shapes=[pltpu.VMEM((tm, D), BF16)],
        compiler_params=_cparams(("parallel", "arbitrary")),
        name="mlp",
    )(*args)


def _conv_kernel(au_ref, ag_ref, bb_ref, bc_ref, bh_ref, dwa_ref, dwab_ref, lng_ref, lnb_ref, cb_ref,
                 pa_ref, pb_ref, sa_ref, sb_ref, xa_ref, xb_ref, *, T, n_pad, wa, wb):
    t = pl.program_id(1)
    ha = 32
    hb = 8

    @pl.when(t > 0)
    def _():
        xa_ref[0:ha, :] = xa_ref[T:T + ha, :]
        xb_ref[0:hb, :] = xb_ref[T:T + hb, :]

    xa_ref[ha:ha + T, :] = au_ref[...] * _sigmoid(ag_ref[...])
    xb_ref[hb:hb + T, :] = bc_ref[...] * bh_ref[...]

    @pl.when(t == 0)
    def _():
        xa_ref[0:ha + n_pad, :] = jnp.zeros((ha + n_pad, xa_ref.shape[1]), F32)
        xb_ref[0:hb + n_pad, :] = jnp.zeros((hb + n_pad, xb_ref.shape[1]), F32)

    def chunk(c, carry):
        r0 = pl.multiple_of(c * SUB, SUB)
        cols = []
        for g in range(xa_ref.shape[1] // LANES):
            ls = slice(g * LANES, (g + 1) * LANES)
            win = xa_ref[pl.ds(r0, ha + SUB), ls]
            a = None
            for rho in range(8):
                taps = [j for j in range(wa) if (ha - (wa - 1) + j) % 8 == rho]
                if not taps:
                    continue
                rot = win if rho == 0 else pltpu.roll(win, ha + SUB - rho, 0)
                for j in taps:
                    o = ha - (wa - 1) + j - rho
                    term = dwa_ref[j:j + 1, ls] * rot[o:o + SUB, :]
                    a = term if a is None else a + term
            cols.append(a)
        acc = jnp.concatenate(cols, axis=1) + dwab_ref[...]
        mu = jnp.mean(acc, axis=-1, keepdims=True)
        cen = acc - mu
        var = jnp.mean(cen * cen, axis=-1, keepdims=True)
        y = cen * lax.rsqrt(var + EPS) * lng_ref[...] + lnb_ref[...]
        pa_ref[pl.ds(r0, SUB), :] = (y * _sigmoid(y)).astype(pa_ref.dtype)
        winb = xb_ref[pl.ds(r0, hb + SUB), :]
        cv = None
        for j in range(wb):
            o = hb - (wb - 1) + j
            term = cb_ref[j:j + 1, :] * winb[o:o + SUB, :]
            cv = term if cv is None else cv + term
        pb_ref[pl.ds(r0, SUB), :] = (bb_ref[pl.ds(r0, SUB), :] * cv).astype(pb_ref.dtype)
        return carry

    lax.fori_loop(0, T // SUB, chunk, 0)

    @pl.when(t == pl.num_programs(1) - 1)
    def _():
        sa_ref[0] = xa_ref[T + ha - (wa - 1):T + ha, :]
        sb_ref[0] = xb_ref[T + hb - (wb - 1):T + hb, :]


def _conv(proj, dw_a, dw_a_bias, ln_g, ln_b, conv_b, *, B, LB, n_pad, W):
    T = SEQ_TILE
    nT = LB // T
    wa, wb = dw_a.shape[0], conv_b.shape[0]
    assert wa - 1 <= 32 and wb - 1 <= 8 and n_pad <= T and n_pad % 8 == 0
    R = proj.shape[0]

    def seg(c):
        return pl.BlockSpec((T, W), lambda b, t: (b * nT + t, c))

    def par(n):
        return pl.BlockSpec((n, W), lambda b, t: (0, 0))

    row = pl.BlockSpec((T, W), lambda b, t: (b * nT + t, 0))
    return pl.pallas_call(
        functools.partial(_conv_kernel, T=T, n_pad=n_pad, wa=wa, wb=wb),
        out_shape=(jax.ShapeDtypeStruct((R, W), BF16), jax.ShapeDtypeStruct((R, W), BF16),
                   jax.ShapeDtypeStruct((B, wa - 1, W), F32), jax.ShapeDtypeStruct((B, wb - 1, W), F32)),
        grid=(B, nT),
        in_specs=[seg(0), seg(1), seg(2), seg(3), seg(4), par(wa), par(1), par(1), par(1), par(wb)],
        out_specs=(row, row,
                   pl.BlockSpec((1, wa - 1, W), lambda b, t: (b, 0, 0)),
                   pl.BlockSpec((1, wb - 1, W), lambda b, t: (b, 0, 0))),
        scratch_shapes=[pltpu.VMEM((T + 32, W), F32), pltpu.VMEM((T + 8, W), F32)],
        compiler_params=_cparams(("parallel", "arbitrary")),
        name="conv",
    )(proj, proj, proj, proj, proj, dw_a, dw_a_bias.reshape(1, W), ln_g.reshape(1, W), ln_b.reshape(1, W), conv_b)


B_DIAG, B_SUB, B_00, B_10, B_X0 = 1, 2, 3, 4, 5


def _rel_bias(qpos, kpos, table, n_buckets):
    n = jnp.maximum(qpos[:, None] - kpos[None, :], 0)
    max_exact = n_buckets // 2
    nf = jnp.maximum(n, 1).astype(F32)
    large = max_exact + (jnp.log(nf / max_exact) / math.log(MAX_DISTANCE / max_exact)
                         * (n_buckets - max_exact)).astype(jnp.int32)
    bucket = jnp.where(n < max_exact, n, jnp.minimum(large, n_buckets - 1))
    onehot = (bucket[:, :, None] == jnp.arange(n_buckets)[None, None, :]).astype(F32)
    bias = jnp.einsum("qkn,nh->hqk", onehot, table.astype(F32), precision=lax.Precision.HIGHEST)
    return bias - table[n_buckets - 1].astype(F32)[:, None, None]


def _bias_bank(table, T, n_pad):
    nb, H = table.shape
    r = jnp.arange(T)
    causal = (r[None, :] <= r[:, None])[None]
    kpad = (r < n_pad)[None, None, :]
    qval = (r >= n_pad)[None, :, None]
    zero = jnp.zeros((H, T, T), F32)
    diag = jnp.where(causal, _rel_bias(r, r, table, nb), NEG)
    sub = _rel_bias(r + T, r, table, nb)
    b00 = jnp.where(kpad & qval, NEG, diag)
    b10 = jnp.where(kpad, NEG, sub)
    bx0 = jnp.where(kpad, NEG, zero)
    return jnp.stack([zero, diag, sub, b00, b10, bx0], axis=0) * LOG2E


def _sublane_all(x, op):
    for sh in (4, 2, 1):
        x = op(x, pltpu.roll(x, sh, 0))
    return x


def _attn_kernel(lam_ref, q_ref, k_ref, v_ref, bias_ref, sub_ref, o_ref, qm_ref, k16_ref, vt_ref, m_ref, l_ref,
                 acc_ref, *, T, LB, dk, dv, scale, out_scale):
    i = pl.program_id(2)
    S8 = 8

    @pl.when(i == 0)
    def _():
        k16_ref[...] = k_ref[...].astype(BF16)
        for c in range(LB // LANES):
            r0 = c * LANES
            vt_ref[r0 // T, :, r0 % T:r0 % T + LANES] = v_ref[r0:r0 + LANES, :].T.astype(BF16)

    q = q_ref[...] * (scale * LOG2E)
    lane = lax.broadcasted_iota(jnp.int32, q.shape, 1)
    qm_ref[0:T, :] = jnp.where(lane < dk, q, 0.0).astype(BF16)
    qm_ref[T:2 * T, :] = jnp.where(lane >= dk, q, 0.0).astype(BF16)

    m_ref[...] = jnp.full(m_ref.shape, NEG, F32)
    l_ref[...] = jnp.zeros(l_ref.shape, F32)
    acc_ref[...] = jnp.zeros(acc_ref.shape, F32)

    def tile(j, bidx, nt=1):
        r0 = pl.multiple_of(j * T, T)
        k = k16_ref[pl.ds(r0, nt * T), :]
        s = lax.dot_general(k, qm_ref[...], (((1,), (1,)), ((), ())), preferred_element_type=F32)
        if bidx is not None:
            bias = bias_ref[bidx, 0]
            s = jnp.concatenate([s[:, 0:T] + bias, s[:, T:2 * T] + bias], axis=1)
        s3 = s.reshape(nt * T // S8, S8, 2 * T)
        m_old = m_ref[...]
        m_new = jnp.maximum(m_old, _sublane_all(jnp.max(s3, axis=0), jnp.maximum))
        alpha = jnp.exp2(m_old - m_new)
        p3 = jnp.exp2(s3 - m_new[None])
        l_ref[...] = alpha * l_ref[...] + jnp.sum(p3, axis=0)
        p = p3.reshape(nt * T, 2 * T).astype(BF16)
        pv = jnp.dot(vt_ref[j], p[0:T], preferred_element_type=F32)
        for u in range(1, nt):
            pv = pv + jnp.dot(vt_ref[j + u], p[u * T:(u + 1) * T], preferred_element_type=F32)
        acc_ref[...] = alpha[None] * acc_ref[...] + pv.reshape(dv // S8, S8, 2 * T)
        m_ref[...] = m_new

    tile(0, jnp.where(i == 0, B_00, jnp.where(i == 1, B_10, B_X0)))

    n_far = jnp.maximum(i - 2, 0)
    n_quad = n_far // 4

    def far(jj, carry):
        tile(1 + 4 * jj, None, nt=4)
        return carry

    lax.fori_loop(0, n_quad, far, 0)

    @pl.when(n_far % 4 >= 2)
    def _():
        tile(1 + 4 * n_quad, None, nt=2)

    @pl.when(n_far % 2 == 1)
    def _():
        tile(i - 2, None)

    @pl.when(i >= 2)
    def _():
        tile(i - 1, B_SUB)

    @pl.when(i >= 1)
    def _():
        tile(i, B_DIAG)

    lam = lam_ref[0]
    on = acc_ref[...] / _sublane_all(l_ref[...], jnp.add)[None]
    o3 = on[:, :, 0:T] - lam * on[:, :, T:2 * T]
    ms = _sublane_all(jnp.sum(o3 * o3, axis=0), jnp.add) * (1.0 / dv)
    y = (o3 * lax.rsqrt(ms + EPS)[None]).reshape(dv, T).T
    o_ref[...] = (y * sub_ref[...] * out_scale).astype(o_ref.dtype)


def _attn(proj, bank, lam, subln, *, B, LB, H, dv, lam_init):
    T = SEQ_TILE
    nQ = LB // T
    R = proj.shape[0]
    dk = dv // 2
    qc, kc, vc = 5 * H, 6 * H, 7 * H
    return pl.pallas_call(
        functools.partial(_attn_kernel, T=T, LB=LB, dk=dk, dv=dv, scale=dk ** -0.5, out_scale=1.0 - lam_init),
        out_shape=jax.ShapeDtypeStruct((R, H * dv), BF16),
        grid=(B, H, nQ),
        in_specs=[pl.BlockSpec(memory_space=pltpu.SMEM),
                  pl.BlockSpec((T, dv), lambda b, h, i: (b * nQ + i, qc + h)),
                  pl.BlockSpec((LB, dv), lambda b, h, i: (b, kc + h)),
                  pl.BlockSpec((LB, dv), lambda b, h, i: (b, vc + h)),
                  pl.BlockSpec((6, 1, T, T), lambda b, h, i: (0, h, 0, 0)),
                  pl.BlockSpec((1, dv), lambda b, h, i: (0, 0))],
        out_specs=pl.BlockSpec((T, dv), lambda b, h, i: (b * nQ + i, h)),
        scratch_shapes=[pltpu.VMEM((2 * T, dv), BF16), pltpu.VMEM((LB, dv), BF16), pltpu.VMEM((nQ, dv, T), BF16),
                        pltpu.VMEM((8, 2 * T), F32), pltpu.VMEM((8, 2 * T), F32),
                        pltpu.VMEM((dv // 8, 8, 2 * T), F32)],
        compiler_params=_cparams(("parallel", "parallel", "arbitrary")),
        name="attn",
    )(lam, proj, proj, proj, jnp.swapaxes(bank, -1, -2), subln.reshape(1, dv))


def _log_forget(zf, log_lb, log1m_lb):
    ls = jnp.minimum(zf, 0.0) - jnp.log1p(jnp.exp(-jnp.abs(zf)))
    b = log1m_lb + ls
    hi = jnp.maximum(log_lb, b)
    lo = jnp.minimum(log_lb, b)
    return hi + jnp.log1p(jnp.exp(lo - hi))


def _split3(x):
    hi = x.astype(BF16)
    r = x - hi.astype(F32)
    mid = r.astype(BF16)
    lo = (r - mid.astype(F32)).astype(BF16)
    return hi, mid, lo


def _hgrn_kernel(q_ref, f_ref, i_ref, g_ref, loglb_ref, log1m_ref, omlb_ref, gn_ref,
                 o_ref, s_ref, st_ref, *, T, n_pad, H, dk):
    t = pl.program_id(1)

    @pl.when(t == 0)
    def _():
        st_ref[...] = jnp.zeros(st_ref.shape, F32)

    U = 8
    rr = lax.broadcasted_iota(jnp.int32, (SUB, SUB), 0)
    cc = lax.broadcasted_iota(jnp.int32, (SUB, SUB), 1)
    tri = jnp.where(cc <= rr, 1.0, 0.0).astype(BF16)
    row = lax.broadcasted_iota(jnp.int32, (SUB, 1), 0)
    urow = lax.broadcasted_iota(jnp.int32, (U, 1), 0)

    def chunk(c, carry):
        r0 = pl.multiple_of(c * SUB, SUB)
        zf = f_ref[pl.ds(r0, SUB), :]
        logf = _log_forget(zf, loglb_ref[...], log1m_ref[...])
        hi, mid, lo = _split3(logf)
        bcum = (jnp.dot(tri, hi, preferred_element_type=F32) + jnp.dot(tri, mid, preferred_element_type=F32)
                + jnp.dot(tri, lo, preferred_element_type=F32))
        kd = omlb_ref[...] * _sigmoid(-zf)
        zq = q_ref[pl.ds(r0, SUB), :]
        qd = zq * _sigmoid(zq)
        seq_row = t * T + r0 + row
        vd = jnp.where(seq_row >= n_pad, i_ref[pl.ds(r0, SUB), :], 0.0)
        zg = g_ref[pl.ds(r0, SUB), :]
        og = zg * _sigmoid(zg)
        blast, bmid = bcum[SUB - 1:SUB, :], bcum[U - 1:U, :]
        qs = (qd * jnp.exp(bcum)).astype(BF16)
        kt = (kd * jnp.exp(blast - bcum)).astype(BF16)
        q1 = (qd[U:SUB] * jnp.exp(bcum[U:SUB] - bmid)).astype(BF16)
        k0 = (kd[0:U] * jnp.exp(bmid - bcum[0:U])).astype(BF16)
        v16 = vd.astype(BF16)
        glast = jnp.exp(blast)
        o_state, a10 = [], []
        for h in range(H):
            sl = slice(h * dk, (h + 1) * dk)
            st = st_ref[h]
            o_state.append(lax.dot_general(qs[:, sl], st.astype(BF16), (((1,), (1,)), ((), ())),
                                           preferred_element_type=F32))
            a10.append(lax.dot_general(q1[:, sl], k0[:, sl], (((1,), (1,)), ((), ())),
                                       preferred_element_type=F32))
            kv = lax.dot_general(v16[:, sl], kt[:, sl], (((0,), (0,)), ((), ())), preferred_element_type=F32)
            st_ref[h] = st * glast[:, sl] + kv
        intra = []
        for h in range(H):
            sl = slice(h * dk, (h + 1) * dk)
            units = []
            for u in range(SUB // U):
                us = slice(u * U, (u + 1) * U)
                bu, qu, ku, vu = bcum[us, sl], qd[us, sl], kd[us, sl], vd[us, sl]
                a = None
                for d in range(U):
                    if d == 0:
                        kr, br, vr = ku, bu, vu
                    else:
                        kr, br, vr = pltpu.roll(ku, d, 0), pltpu.roll(bu, d, 0), pltpu.roll(vu, d, 0)
                    w = jnp.where(urow >= d, qu * kr * jnp.exp(bu - br), 0.0)
                    term = jnp.sum(w, axis=-1, keepdims=True) * vr
                    a = term if a is None else a + term
                units.append(a)
            intra.append(jnp.concatenate(units, axis=0))
        for h in range(H):
            sl = slice(h * dk, (h + 1) * dk)
            cross = jnp.dot(a10[h].astype(BF16), v16[0:U, sl], preferred_element_type=F32)
            o = o_state[h] + intra[h] + jnp.concatenate([jnp.zeros((U, dk), F32), cross], axis=0)
            o_ref[pl.ds(r0, SUB), sl] = (_rms(o, gn_ref[...]) * og[:, sl]).astype(o_ref.dtype)
        return carry

    lax.fori_loop(0, T // SUB, chunk, 0)

    @pl.when(t == pl.num_programs(1) - 1)
    def _():
        for h in range(H):
            s_ref[0, h] = st_ref[h].T


def _hgrn(proj, log_lb, log1m_lb, om_lb, g_norm, *, B, LB, n_pad, H, dk):
    T = SEQ_TILE
    nT = LB // T
    R = proj.shape[0]
    W = H * dk

    def seg(c):
        return pl.BlockSpec((T, W), lambda b, t: (b * nT + t, c))

    def par(n):
        return pl.BlockSpec((1, n), lambda b, t: (0, 0))

    return pl.pallas_call(
        functools.partial(_hgrn_kernel, T=T, n_pad=n_pad, H=H, dk=dk),
        out_shape=(jax.ShapeDtypeStruct((R, W), BF16), jax.ShapeDtypeStruct((B, H, dk, dk), F32)),
        grid=(B, nT),
        in_specs=[seg(8), seg(9), seg(10), seg(11), par(W), par(W), par(W), par(dk)],
        out_specs=(pl.BlockSpec((T, W), lambda b, t: (b * nT + t, 0)),
                   pl.BlockSpec((1, H, dk, dk), lambda b, t: (b, 0, 0, 0))),
        scratch_shapes=[pltpu.VMEM((H, dk, dk), F32)],
        compiler_params=_cparams(("parallel", "arbitrary")),
        name="hgrn",
    )(proj, proj, proj, proj, log_lb, log1m_lb, om_lb, g_norm.reshape(1, dk))


def _sample_mix_kernel(p_ref, sa_ref, sb_ref, sh_ref, dwa_ref, dwab_ref, lng_ref, lnb_ref, cb_ref,
                       lb_ref, omlb_ref, gn_ref, pa_in, pb_in, pd_in,
                       pa_ref, pb_ref, pd_ref, nsa_ref, nsb_ref, nsh_ref, o_scr, *, NB, W, H, dk, wa, wb):
    del pa_in, pb_in, pd_in

    def seg(c):
        return p_ref[:, c * W:(c + 1) * W]

    pad = jnp.zeros((SUB - NB, W), F32)

    glu = seg(0) * _sigmoid(seg(1))
    rows = []
    for b in range(NB):
        cv = jnp.sum(sa_ref[b] * dwa_ref[0:wa - 1, :], axis=0, keepdims=True)
        rows.append(cv + dwa_ref[wa - 1:wa, :] * glu[b:b + 1, :])
        nsa_ref[b, 0:wa - 2, :] = sa_ref[b, 1:wa - 1, :]
        nsa_ref[b, wa - 2:wa - 1, :] = glu[b:b + 1, :]
    acc = jnp.concatenate(rows, axis=0) + dwab_ref[...]
    mu = jnp.mean(acc, axis=-1, keepdims=True)
    cen = acc - mu
    var = jnp.mean(cen * cen, axis=-1, keepdims=True)
    y = cen * lax.rsqrt(var + EPS) * lng_ref[...] + lnb_ref[...]
    pa_ref[...] = jnp.concatenate([y * _sigmoid(y), pad], axis=0).astype(pa_ref.dtype)

    u = seg(3) * seg(4)
    rows = []
    for b in range(NB):
        cv = jnp.sum(sb_ref[b] * cb_ref[0:wb - 1, :], axis=0, keepdims=True)
        rows.append(cv + cb_ref[wb - 1:wb, :] * u[b:b + 1, :])
        if wb > 2:
            nsb_ref[b, 0:wb - 2, :] = sb_ref[b, 1:wb - 1, :]
        nsb_ref[b, wb - 2:wb - 1, :] = u[b:b + 1, :]
    pb_ref[...] = jnp.concatenate([seg(2) * jnp.concatenate(rows, axis=0), pad], axis=0).astype(pb_ref.dtype)

    zf = seg(9)
    sg = _sigmoid(zf)
    fg = lb_ref[...] + omlb_ref[...] * sg
    kd = omlb_ref[...] * _sigmoid(-zf)
    zq = seg(8)
    qd = zq * _sigmoid(zq)
    vd = seg(10)
    eye = lax.broadcasted_iota(jnp.int32, (dk, dk), 0) == lax.broadcasted_iota(jnp.int32, (dk, dk), 1)

    def col(x):
        return jnp.sum(jnp.where(eye, x, 0.0), axis=1, keepdims=True)

    for b in range(NB):
        for h in range(H):
            sl = slice(h * dk, (h + 1) * dk)
            s_new = col(fg[b:b + 1, sl]) * sh_ref[b, h] + col(kd[b:b + 1, sl]) * vd[b:b + 1, sl]
            nsh_ref[b, h] = s_new
            o_scr[b:b + 1, sl] = jnp.sum(col(qd[b:b + 1, sl]) * s_new, axis=0, keepdims=True)
    zg = seg(11)
    og = zg * _sigmoid(zg)
    o = o_scr[...]
    outs = [_rms(o[:, h * dk:(h + 1) * dk], gn_ref[...]) for h in range(H)]
    pd_ref[...] = jnp.concatenate([jnp.concatenate(outs, axis=1) * og, pad], axis=0).astype(pd_ref.dtype)


def _sample_mix(proj, sa, sb, sh, dw_a, dw_a_bias, ln_g, ln_b, conv_b, lb, om_lb, g_norm, pre_a, pre_b, pre_d):
    NB, wa1, W = sa.shape
    wb1 = sb.shape[1]
    H, dk = sh.shape[1], sh.shape[2]
    R = proj.shape[0]

    def full(shape):
        return pl.BlockSpec(shape, lambda i: (0,) * len(shape))

    rows = pl.BlockSpec((SUB, W), lambda i: (0, 0))
    return pl.pallas_call(
        functools.partial(_sample_mix_kernel, NB=NB, W=W, H=H, dk=dk, wa=wa1 + 1, wb=wb1 + 1),
        out_shape=(jax.ShapeDtypeStruct((R, W), BF16), jax.ShapeDtypeStruct((R, W), BF16),
                   jax.ShapeDtypeStruct((R, W), BF16), jax.ShapeDtypeStruct(sa.shape, F32),
                   jax.ShapeDtypeStruct(sb.shape, F32), jax.ShapeDtypeStruct(sh.shape, F32)),
        grid=(1,),
        in_specs=[pl.BlockSpec((NB, N_SEG * W), lambda i: (0, 0)), full(sa.shape), full(sb.shape), full(sh.shape),
                  full((wa1 + 1, W)), full((1, W)), full((1, W)), full((1, W)), full((wb1 + 1, W)),
                  full((1, W)), full((1, W)), full((1, dk)),
                  pl.BlockSpec(memory_space=pl.ANY), pl.BlockSpec(memory_space=pl.ANY),
                  pl.BlockSpec(memory_space=pl.ANY)],
        out_specs=(rows, rows, rows, full(sa.shape), full(sb.shape), full(sh.shape)),
        scratch_shapes=[pltpu.VMEM((NB, W), F32)],
        input_output_aliases={12: 0, 13: 1, 14: 2},
        compiler_params=_cparams(("arbitrary",)),
        name="sample_mix",
    )(proj, sa, sb, sh, dw_a, dw_a_bias.reshape(1, W), ln_g.reshape(1, W), ln_b.reshape(1, W), conv_b,
      lb, om_lb, g_norm.reshape(1, dk), pre_a, pre_b, pre_d)


def _decode_kernel(pt_ref, lam_ref, q_ref, kn_ref, vn_ref, bias_ref, biasn_ref, sub_ref, pc_in, *rest,
                   G, P, H, dv, scale, out_scale, NB):
    del pt_ref, pc_in
    k_refs, v_refs = rest[:G], rest[G:2 * G]
    o_ref, m_ref, l_ref, acc_ref, o_scr = rest[2 * G:]
    b = pl.program_id(0)
    s = pl.program_id(1)
    dk = dv // 2
    HM = 2 * H
    W = H * dv
    C = P * H
    last = s == pl.num_programs(1) - 1

    @pl.when(s == 0)
    def _():
        m_ref[...] = jnp.full(m_ref.shape, NEG, F32)
        l_ref[...] = jnp.zeros(l_ref.shape, F32)
        acc_ref[...] = jnp.zeros(acc_ref.shape, F32)

    def per_map_rows(row):
        return jnp.concatenate([row[:, h * dv:(h + 1) * dv] for h in range(H) for _ in range(2)], axis=0)

    rr = lax.broadcasted_iota(jnp.int32, (HM, dv), 0)
    ll = lax.broadcasted_iota(jnp.int32, (HM, dv), 1)
    own_map = (ll >= dk) == (jnp.bitwise_and(rr, 1) == 1)
    qall = jnp.where(own_map, per_map_rows(q_ref[pl.ds(b, 1), :]) * (scale * LOG2E), 0.0)
    qall16 = qall.astype(BF16)
    col_head = jnp.bitwise_and(lax.broadcasted_iota(jnp.int32, (HM, C), 1), H - 1)
    row_head = lax.shift_right_logical(lax.broadcasted_iota(jnp.int32, (HM, C), 0), 1)
    own_head = col_head == row_head

    blocks = []
    for g in range(G):
        kg = k_refs[g][...].reshape(C, dv).astype(BF16)
        sg = lax.dot_general(qall16, kg, (((1,), (1,)), ((), ())), preferred_element_type=F32)
        if g == G - 1:
            sg = sg + jnp.where(last, bias_ref[...], 0.0)
        blocks.append(jnp.where(own_head, sg, NEG))
    sc = jnp.concatenate(blocks, axis=1)
    m_old = m_ref[...]
    m_new = jnp.maximum(m_old, jnp.max(sc, axis=-1, keepdims=True))
    alpha = jnp.exp2(m_old - m_new)
    p = jnp.exp2(sc - m_new)
    l_ref[...] = alpha * l_ref[...] + jnp.sum(p, axis=-1, keepdims=True)
    pv = None
    for g in range(G):
        vg = v_refs[g][...].reshape(C, dv).astype(BF16)
        y = jnp.dot(p[:, g * C:(g + 1) * C].astype(BF16), vg, preferred_element_type=F32)
        pv = y if pv is None else pv + y
    acc_ref[...] = alpha * acc_ref[...] + pv
    m_ref[...] = m_new

    @pl.when(last)
    def _():
        kn = per_map_rows(kn_ref[pl.ds(b, 1), :])
        vn = per_map_rows(vn_ref[pl.ds(b, 1), :])
        sn = jnp.sum(qall * kn, axis=-1, keepdims=True) + biasn_ref[...]
        m1 = m_ref[...]
        m2 = jnp.maximum(m1, sn)
        a2 = jnp.exp2(m1 - m2)
        pn = jnp.exp2(sn - m2)
        o = (a2 * acc_ref[...] + pn * vn) / (a2 * l_ref[...] + pn)
        lam = lam_ref[0]
        outs = []
        for h in range(H):
            oh = o[2 * h:2 * h + 1, :] - lam * o[2 * h + 1:2 * h + 2, :]
            outs.append(_rms(oh, sub_ref[...]) * out_scale)
        o_scr[pl.ds(b, 1), :] = jnp.concatenate(outs, axis=1)

    @pl.when((s == pl.num_programs(1) - 1) & (b == NB - 1))
    def _():
        o_ref[...] = jnp.concatenate([o_scr[...], jnp.zeros((SUB - NB, W), F32)], axis=0).astype(o_ref.dtype)


def _decode_attn(proj, cache_k, cache_v, page_table, layer, bias_past, bias_new, lam, subln, pre_c,
                 *, H, dv, lam_init):
    NB, n_pages = page_table.shape
    P = cache_k.shape[2]
    G = PAGES_PER_STEP
    assert n_pages % G == 0 and P >= MAX_DISTANCE and H & (H - 1) == 0
    W = H * dv
    HM = 2 * H

    def page(g):
        return pl.BlockSpec((None, None, P, H, dv), lambda b, s, pt: (pt[b, s * G + g], layer, 0, 0, 0))

    def rows(c):
        return pl.BlockSpec((NB, W), lambda b, s, pt: (0, c))

    grid_spec = pltpu.PrefetchScalarGridSpec(
        num_scalar_prefetch=1,
        grid=(NB, n_pages // G),
        in_specs=[pl.BlockSpec(memory_space=pltpu.SMEM), rows(5), rows(6), rows(7),
                  pl.BlockSpec((HM, P * H), lambda b, s, pt: (0, 0)),
                  pl.BlockSpec((HM, 1), lambda b, s, pt: (0, 0)),
                  pl.BlockSpec((1, dv), lambda b, s, pt: (0, 0)),
                  pl.BlockSpec(memory_space=pl.ANY)]
        + [page(g) for g in range(G)] + [page(g) for g in range(G)],
        out_specs=pl.BlockSpec((SUB, W), lambda b, s, pt: (0, 0)),
        scratch_shapes=[pltpu.VMEM((HM, 1), F32), pltpu.VMEM((HM, 1), F32), pltpu.VMEM((HM, dv), F32),
                        pltpu.VMEM((NB, W), F32)],
    )
    return pl.pallas_call(
        functools.partial(_decode_kernel, G=G, P=P, H=H, dv=dv, scale=(dv // 2) ** -0.5,
                          out_scale=1.0 - lam_init, NB=NB),
        out_shape=jax.ShapeDtypeStruct(pre_c.shape, pre_c.dtype),
        grid_spec=grid_spec,
        input_output_aliases={8: 0},
        compiler_params=_cparams(("arbitrary", "arbitrary")),
        name="decode_attn",
    )(page_table, lam, proj, proj, proj, bias_past, bias_new, subln.reshape(1, dv), pre_c,
      *([cache_k] * G), *([cache_v] * G))


def kernel(x_prompt, x_sample, cache_k, cache_v, state_conv_a, state_conv_b, state_hgrn, page_table, meta_tokens, rel_bias_table, hgrn_lower_bound, norm1, w_in, dw_a, dw_a_bias, ln_a_g, ln_a_b, w_a_out, conv_b, w_b_out, lam_q1, lam_k1, lam_q2, lam_k2, subln, w_c_out, g_norm_d, w_d_out, w_gate, b_gate, w_o, norm2, w_up, w_down, final_norm):
    B, L, D = x_prompt.shape
    NB = x_sample.shape[0]
    assert x_sample.shape[1] == 1
    depth = w_in.shape[0]
    n_meta = meta_tokens.shape[0]
    W = state_conv_a.shape[-1]
    H, dv = cache_v.shape[3], cache_v.shape[4]
    Hr, dkr = state_hgrn.shape[2], state_hgrn.shape[3]
    n_buckets = rel_bias_table.shape[0]
    past_len = page_table.shape[1] * cache_k.shape[2]
    assert w_in.shape[2] == N_SEG * W and H * dv == W and Hr * dkr == W and state_conv_b.shape[-1] == W
    assert cache_k.shape[4] == dv and state_hgrn.shape[4] == dkr and dv == LANES and dkr == LANES

    T = SEQ_TILE
    Lt = n_meta + L
    LB = -(-(Lt + NB) // T) * T
    n_pad = LB - Lt
    R = B * LB
    tm = ROW_TILE if R % ROW_TILE == 0 else T
    assert n_pad <= T and NB <= SUB <= n_pad and R % tm == 0

    meta = jnp.broadcast_to(meta_tokens[None].astype(F32), (B, n_meta, D))
    h = jnp.concatenate([jnp.zeros((B, n_pad, D), F32), meta, x_prompt], axis=1)
    h = h.at[0, :NB].set(x_sample[:, 0]).reshape(R, D)

    lbs = jnp.cumsum(jax.nn.softmax(hgrn_lower_bound.astype(F32), axis=0), axis=0)
    lbs = lbs - lbs[0:1]
    bank = _bias_bank(rel_bias_table, T, n_pad)
    pos_s = jnp.full((1,), past_len, jnp.int32)
    P = cache_k.shape[2]
    kpos_near = jnp.arange(past_len - P, past_len + 1)
    bias_dec = jnp.repeat(_rel_bias(pos_s, kpos_near, rel_bias_table, n_buckets)[:, 0, :], 2, axis=0) * LOG2E
    bias_past, bias_new = jnp.repeat(bias_dec[:, :P], H, axis=1), bias_dec[:, P:]

    w_in16, w_gate16, w_o16 = w_in.astype(BF16), w_gate.astype(BF16), w_o.astype(BF16)
    w_up16, w_down16 = w_up.astype(BF16), w_down.astype(BF16)
    w_branch16 = jnp.stack([w_a_out, w_b_out, w_c_out, w_d_out], axis=1).astype(BF16)

    k_p, v_p, ca_p, cb_p, s_p = [], [], [], [], []
    k_s, v_s, ca_s, cb_s, s_s = [], [], [], [], []
    for l in range(depth):
        lam_init = 0.8 - 0.6 * math.exp(-0.3 * l)
        lam = (jnp.exp(jnp.sum(lam_q1[l].astype(F32) * lam_k1[l].astype(F32)))
               - jnp.exp(jnp.sum(lam_q2[l].astype(F32) * lam_k2[l].astype(F32))) + lam_init).reshape(1)
        lb = lbs[l].reshape(1, W)
        log_lb, log1m_lb, om_lb = jnp.log(lb), jnp.log1p(-lb), 1.0 - lb

        proj = _matmul(h, w_in16, l, tm=tm, tn=COL_TILE, out_dtype=F32, norm_g=norm1[l], name="proj")
        gates = _matmul(h, w_gate16, l, tm=tm, tn=COL_TILE, out_dtype=BF16, norm_g=norm1[l],
                        bias=b_gate[l], act="sigmoid", name="gates")

        pre_a, pre_b, ca, cb = _conv(proj, dw_a[l], dw_a_bias[l], ln_a_g[l], ln_a_b[l], conv_b[l],
                                     B=B, LB=LB, n_pad=n_pad, W=W)
        pre_c = _attn(proj, bank, lam, subln[l], B=B, LB=LB, H=H, dv=dv, lam_init=lam_init)
        pre_d, s_fin = _hgrn(proj, log_lb, log1m_lb, om_lb, g_norm_d[l], B=B, LB=LB, n_pad=n_pad, H=Hr, dk=dkr)

        pre_a, pre_b, pre_d, nsa, nsb, nsh = _sample_mix(
            proj, state_conv_a[l], state_conv_b[l], state_hgrn[l], dw_a[l], dw_a_bias[l], ln_a_g[l], ln_a_b[l],
            conv_b[l], lb, om_lb, g_norm_d[l], pre_a, pre_b, pre_d)
        pre_c = _decode_attn(proj, cache_k, cache_v, page_table, l, bias_past, bias_new, lam, subln[l], pre_c,
                             H=H, dv=dv, lam_init=lam_init)

        merged = _merge((pre_a, pre_b, pre_c, pre_d), w_branch16, l, gates, tm=tm, tn=MERGE_COL_TILE)
        h = _matmul(merged, w_o16, l, tm=tm, tn=COL_TILE, out_dtype=F32, residual=h, name="w_o")
        last = l == depth - 1
        h = _mlp(h, norm2[l], w_up16, w_down16, l, final_norm if last else None, tm=tm, tf=FF_TILE)

        proj3 = proj.reshape(B, LB, N_SEG * W)
        k_p.append(proj3[:, n_pad:, 6 * W:7 * W])
        v_p.append(proj3[:, n_pad:, 7 * W:8 * W])
        k_s.append(proj[:NB, 6 * W:7 * W])
        v_s.append(proj[:NB, 7 * W:8 * W])
        ca_p.append(ca); cb_p.append(cb); s_p.append(s_fin)
        ca_s.append(nsa); cb_s.append(nsb); s_s.append(nsh)

    y3 = h.reshape(B, LB, D)
    y_prompt = y3[:, n_pad + n_meta:]
    y_sample = y3[0, :NB].reshape(NB, 1, D)
    new_k_prompt = jnp.stack(k_p, axis=2).reshape(B, Lt, depth, H, dv)
    new_v_prompt = jnp.stack(v_p, axis=2).reshape(B, Lt, depth, H, dv)
    new_k_sample = jnp.stack(k_s, axis=1).reshape(NB, 1, depth, H, dv)
    new_v_sample = jnp.stack(v_s, axis=1).reshape(NB, 1, depth, H, dv)
    return (y_prompt, y_sample, new_k_prompt, new_v_prompt, jnp.stack(ca_p, 0), jnp.stack(cb_p, 0),
            jnp.stack(s_p, 0), new_k_sample, new_v_sample, jnp.stack(ca_s, 0), jnp.stack(cb_s, 0),
            jnp.stack(s_s, 0))
```

```python
import functools
import math

import jax
import jax.numpy as jnp
from jax import lax
from jax.experimental import pallas as pl
from jax.experimental.pallas import tpu as pltpu

F32 = jnp.float32
BF16 = jnp.bfloat16

EPS = 1e-6
MAX_DISTANCE = 128
NEG = -1e30
LOG2E = math.log2(math.e)

LANES = 128
SEQ_TILE = 384
ROW_TILE = 768
COL_TILE = 1024
MERGE_COL_TILE = 512
FF_TILE = 1024
SUB = 16
PAGES_PER_STEP = 8
VMEM_LIMIT = 56 * 1024 * 1024
N_SEG = 12
N_BRANCH = 4


def _cparams(sem):
    return pltpu.CompilerParams(dimension_semantics=sem, vmem_limit_bytes=VMEM_LIMIT)


def _sigmoid(x):
    return 1.0 / (1.0 + jnp.exp(-x))


def _rms(x, g):
    return x * lax.rsqrt(jnp.mean(x * x, axis=-1, keepdims=True) + EPS) * g


def _mm_kernel(*refs, has_norm, has_bias, act, has_res, heads):
    it = iter(refs)
    x_ref = next(it)
    g_ref = next(it) if has_norm else None
    w_ref = next(it)
    b_ref = next(it) if has_bias else None
    r_ref = next(it) if has_res else None
    o_ref = next(it)
    hv_ref = next(it) if heads else None
    if has_norm:
        xn_ref = next(it)

        @pl.when(pl.program_id(1) == 0)
        def _():
            xn_ref[...] = _rms(x_ref[...], g_ref[...]).astype(BF16)

        lhs = xn_ref[...]
    else:
        lhs = x_ref[...]
    y = jnp.dot(lhs, w_ref[...], preferred_element_type=F32)
    if has_bias:
        y = y + b_ref[...]
    if act == "sigmoid":
        y = _sigmoid(y)
    if has_res:
        y = y + r_ref[...]
    o_ref[...] = y.astype(o_ref.dtype)
    if heads:
        j0, n, H, dv = heads
        j = pl.program_id(1)

        @pl.when((j >= j0) & (j < j0 + n))
        def _():
            for h in range(H):
                hv_ref[:, h, :] = y[:, h * dv:(h + 1) * dv]


def _matmul(x, w, layer, *, tm, tn, out_dtype, norm_g=None, bias=None, act=None, residual=None, heads=None, name):
    M, K = x.shape
    N = w.shape[2]
    tn = min(tn, N)
    if heads is not None:
        tn = heads[2] * heads[3]
    assert M % tm == 0 and N % tn == 0
    args = [x]
    specs = [pl.BlockSpec((tm, K), lambda i, j: (i, 0))]
    scratch = []
    if norm_g is not None:
        args.append(norm_g.reshape(1, K))
        specs.append(pl.BlockSpec((1, K), lambda i, j: (0, 0)))
        scratch.append(pltpu.VMEM((tm, K), BF16))
    args.append(w)
    specs.append(pl.BlockSpec((None, K, tn), lambda i, j: (layer, 0, j)))
    if bias is not None:
        args.append(bias.reshape(1, N))
        specs.append(pl.BlockSpec((1, tn), lambda i, j: (0, j)))
    if residual is not None:
        args.append(residual)
        specs.append(pl.BlockSpec((tm, tn), lambda i, j: (i, j)))
    kern = functools.partial(_mm_kernel, has_norm=norm_g is not None, has_bias=bias is not None,
                             act=act, has_res=residual is not None, heads=heads)
    out_shape = jax.ShapeDtypeStruct((M, N), out_dtype)
    out_specs = pl.BlockSpec((tm, tn), lambda i, j: (i, j))
    if heads is not None:
        j0, n, H, dv = heads
        out_shape = (out_shape, jax.ShapeDtypeStruct((n, M, H, dv), out_dtype))
        out_specs = (out_specs, pl.BlockSpec((None, tm, H, dv), lambda i, j: (jnp.clip(j - j0, 0, n - 1), i, 0, 0)))
    return pl.pallas_call(
        kern,
        out_shape=out_shape,
        grid=(M // tm, N // tn),
        in_specs=specs,
        out_specs=out_specs,
        scratch_shapes=scratch,
        compiler_params=_cparams(("parallel", "arbitrary")),
        name=name,
    )(*args)


def _merge_kernel(pa_ref, pb_ref, pc_ref, pd_ref, w_ref, g0_ref, g1_ref, g2_ref, g3_ref, o_ref):
    acc = None
    for k, (p_ref, g_ref) in enumerate(((pa_ref, g0_ref), (pb_ref, g1_ref), (pc_ref, g2_ref), (pd_ref, g3_ref))):
        y = jnp.dot(p_ref[...], w_ref[k], preferred_element_type=F32) * g_ref[...].astype(F32)
        acc = y if acc is None else acc + y
    o_ref[...] = acc.astype(o_ref.dtype)


def _merge(pres, w_branch, layer, gates, *, tm, tn):
    R, W = pres[0].shape
    D = w_branch.shape[3]
    tn = min(tn, D)
    n_col = D // tn
    pre_spec = pl.BlockSpec((tm, W), lambda i, j: (i, 0))
    gate_specs = [pl.BlockSpec((tm, tn), functools.partial(lambda i, j, b: (i, b * n_col + j), b=b))
                  for b in range(N_BRANCH)]
    return pl.pallas_call(
        _merge_kernel,
        out_shape=jax.ShapeDtypeStruct((R, D), BF16),
        grid=(R // tm, n_col),
        in_specs=([pre_spec] * N_BRANCH
                  + [pl.BlockSpec((None, N_BRANCH, W, tn), lambda i, j: (layer, 0, 0, j))] + gate_specs),
        out_specs=pl.BlockSpec((tm, tn), lambda i, j: (i, j)),
        compiler_params=_cparams(("parallel", "arbitrary")),
        name="merge",
    )(*pres, w_branch, gates, gates, gates, gates)


def _mlp_kernel(*refs, final):
    if final:
        h_ref, g_ref, wu_ref, wd_ref, fn_ref, o_ref, hn_ref = refs
    else:
        h_ref, g_ref, wu_ref, wd_ref, o_ref, hn_ref = refs
    f = pl.program_id(1)

    @pl.when(f == 0)
    def _():
        hn_ref[...] = _rms(h_ref[...], g_ref[...]).astype(BF16)
        o_ref[...] = h_ref[...]

    u = jnp.dot(hn_ref[...], wu_ref[...], preferred_element_type=F32)
    u = jnp.square(jnp.maximum(u, 0.0))
    o_ref[...] += jnp.dot(u.astype(BF16), wd_ref[...], preferred_element_type=F32)

    if final:
        @pl.when(f == pl.num_programs(1) - 1)
        def _():
            o_ref[...] = _rms(o_ref[...], fn_ref[...])


def _mlp(h, norm_g, w_up, w_down, layer, final_g, *, tm, tf):
    R, D = h.shape
    FF = w_up.shape[2]
    final = final_g is not None
    args = [h, norm_g.reshape(1, D), w_up, w_down]
    specs = [pl.BlockSpec((tm, D), lambda i, f: (i, 0)),
             pl.BlockSpec((1, D), lambda i, f: (0, 0)),
             pl.BlockSpec((None, D, tf), lambda i, f: (layer, 0, f)),
             pl.BlockSpec((None, tf, D), lambda i, f: (layer, f, 0))]
    if final:
        args.append(final_g.reshape(1, D))
        specs.append(pl.BlockSpec((1, D), lambda i, f: (0, 0)))
    return pl.pallas_call(
        functools.partial(_mlp_kernel, final=final),
        out_shape=jax.ShapeDtypeStruct((R, D), F32),
        grid=(R // tm, FF // tf),
        in_specs=specs,
        out_specs=pl.BlockSpec((tm, D), lambda i, f: (i, 0)),
        scratch_shapes=[pltpu.VMEM((tm, D), BF16)],
        compiler_params=_cparams(("parallel", "arbitrary")),
        name="mlp",
    )(*args)


def _conv_kernel(au_ref, ag_ref, bb_ref, bc_ref, bh_ref, dwa_ref, dwab_ref, lng_ref, lnb_ref, cb_ref,
                 pa_ref, pb_ref, sa_ref, sb_ref, xa_ref, xb_ref, *, T, n_pad, wa, wb):
    t = pl.program_id(1)
    ha = 32
    hb = 8

    @pl.when(t > 0)
    def _():
        xa_ref[0:ha, :] = xa_ref[T:T + ha, :]
        xb_ref[0:hb, :] = xb_ref[T:T + hb, :]

    xa_ref[ha:ha + T, :] = au_ref[...] * _sigmoid(ag_ref[...])
    xb_ref[hb:hb + T, :] = bc_ref[...] * bh_ref[...]

    @pl.when(t == 0)
    def _():
        xa_ref[0:ha + n_pad, :] = jnp.zeros((ha + n_pad, xa_ref.shape[1]), F32)
        xb_ref[0:hb + n_pad, :] = jnp.zeros((hb + n_pad, xb_ref.shape[1]), F32)

    def chunk(c, carry):
        r0 = pl.multiple_of(c * SUB, SUB)
        cols = []
        for g in range(xa_ref.shape[1] // LANES):
            ls = slice(g * LANES, (g + 1) * LANES)
            win = xa_ref[pl.ds(r0, ha + SUB), ls]
            a = None
            for rho in range(8):
                taps = [j for j in range(wa) if (ha - (wa - 1) + j) % 8 == rho]
                if not taps:
                    continue
                rot = win if rho == 0 else pltpu.roll(win, ha + SUB - rho, 0)
                for j in taps:
                    o = ha - (wa - 1) + j - rho
                    term = dwa_ref[j:j + 1, ls] * rot[o:o + SUB, :]
                    a = term if a is None else a + term
            cols.append(a)
        acc = jnp.concatenate(cols, axis=1) + dwab_ref[...]
        mu = jnp.mean(acc, axis=-1, keepdims=True)
        cen = acc - mu
        var = jnp.mean(cen * cen, axis=-1, keepdims=True)
        y = cen * lax.rsqrt(var + EPS) * lng_ref[...] + lnb_ref[...]
        pa_ref[pl.ds(r0, SUB), :] = (y * _sigmoid(y)).astype(pa_ref.dtype)
        winb = xb_ref[pl.ds(r0, hb + SUB), :]
        cv = None
        for j in range(wb):
            o = hb - (wb - 1) + j
            term = cb_ref[j:j + 1, :] * winb[o:o + SUB, :]
            cv = term if cv is None else cv + term
        pb_ref[pl.ds(r0, SUB), :] = (bb_ref[pl.ds(r0, SUB), :] * cv).astype(pb_ref.dtype)
        return carry

    lax.fori_loop(0, T // SUB, chunk, 0)

    @pl.when(t == pl.num_programs(1) - 1)
    def _():
        sa_ref[0] = xa_ref[T + ha - (wa - 1):T + ha, :]
        sb_ref[0] = xb_ref[T + hb - (wb - 1):T + hb, :]


def _conv(proj, dw_a, dw_a_bias, ln_g, ln_b, conv_b, *, B, LB, n_pad, W):
    T = SEQ_TILE
    nT = LB // T
    wa, wb = dw_a.shape[0], conv_b.shape[0]
    assert wa - 1 <= 32 and wb - 1 <= 8 and n_pad <= T and n_pad % 8 == 0
    R = proj.shape[0]

    def seg(c):
        return pl.BlockSpec((T, W), lambda b, t: (b * nT + t, c))

    def par(n):
        return pl.BlockSpec((n, W), lambda b, t: (0, 0))

    row = pl.BlockSpec((T, W), lambda b, t: (b * nT + t, 0))
    return pl.pallas_call(
        functools.partial(_conv_kernel, T=T, n_pad=n_pad, wa=wa, wb=wb),
        out_shape=(jax.ShapeDtypeStruct((R, W), BF16), jax.ShapeDtypeStruct((R, W), BF16),
                   jax.ShapeDtypeStruct((B, wa - 1, W), F32), jax.ShapeDtypeStruct((B, wb - 1, W), F32)),
        grid=(B, nT),
        in_specs=[seg(0), seg(1), seg(2), seg(3), seg(4), par(wa), par(1), par(1), par(1), par(wb)],
        out_specs=(row, row,
                   pl.BlockSpec((1, wa - 1, W), lambda b, t: (b, 0, 0)),
                   pl.BlockSpec((1, wb - 1, W), lambda b, t: (b, 0, 0))),
        scratch_shapes=[pltpu.VMEM((T + 32, W), F32), pltpu.VMEM((T + 8, W), F32)],
        compiler_params=_cparams(("parallel", "arbitrary")),
        name="conv",
    )(proj, proj, proj, proj, proj, dw_a, dw_a_bias.reshape(1, W), ln_g.reshape(1, W), ln_b.reshape(1, W), conv_b)


B_DIAG, B_SUB, B_00, B_10, B_X0 = 1, 2, 3, 4, 5


def _rel_bias(qpos, kpos, table, n_buckets):
    n = jnp.maximum(qpos[:, None] - kpos[None, :], 0)
    max_exact = n_buckets // 2
    nf = jnp.maximum(n, 1).astype(F32)
    large = max_exact + (jnp.log(nf / max_exact) / math.log(MAX_DISTANCE / max_exact)
                         * (n_buckets - max_exact)).astype(jnp.int32)
    bucket = jnp.where(n < max_exact, n, jnp.minimum(large, n_buckets - 1))
    onehot = (bucket[:, :, None] == jnp.arange(n_buckets)[None, None, :]).astype(F32)
    bias = jnp.einsum("qkn,nh->hqk", onehot, table.astype(F32), precision=lax.Precision.HIGHEST)
    return bias - table[n_buckets - 1].astype(F32)[:, None, None]


def _bias_bank(table, T, n_pad):
    nb, H = table.shape
    r = jnp.arange(T)
    causal = (r[None, :] <= r[:, None])[None]
    kpad = (r < n_pad)[None, None, :]
    qval = (r >= n_pad)[None, :, None]
    zero = jnp.zeros((H, T, T), F32)
    diag = jnp.where(causal, _rel_bias(r, r, table, nb), NEG)
    sub = _rel_bias(r + T, r, table, nb)
    b00 = jnp.where(kpad & qval, NEG, diag)
    b10 = jnp.where(kpad, NEG, sub)
    bx0 = jnp.where(kpad, NEG, zero)
    return jnp.stack([zero, diag, sub, b00, b10, bx0], axis=0) * LOG2E


def _sublane_all(x, op):
    for sh in (4, 2, 1):
        x = op(x, pltpu.roll(x, sh, 0))
    return x


def _attn_kernel(lam_ref, q_ref, k_ref, v_ref, bias_ref, sub_ref, o_ref, qm_ref, k16_ref, vt_ref, m_ref, l_ref,
                 acc_ref, *, T, LB, dk, dv, scale, out_scale):
    i = pl.program_id(2)
    S8 = 8

    @pl.when(i == 0)
    def _():
        k16_ref[...] = k_ref[...].astype(BF16)
        for c in range(LB // LANES):
            r0 = c * LANES
            vt_ref[r0 // T, :, r0 % T:r0 % T + LANES] = v_ref[r0:r0 + LANES, :].T.astype(BF16)

    q = q_ref[...] * (scale * LOG2E)
    lane = lax.broadcasted_iota(jnp.int32, q.shape, 1)
    qm_ref[0:T, :] = jnp.where(lane < dk, q, 0.0).astype(BF16)
    qm_ref[T:2 * T, :] = jnp.where(lane >= dk, q, 0.0).astype(BF16)

    m_ref[...] = jnp.full(m_ref.shape, NEG, F32)
    l_ref[...] = jnp.zeros(l_ref.shape, F32)
    acc_ref[...] = jnp.zeros(acc_ref.shape, F32)

    def tile(j, bidx, nt=1):
        r0 = pl.multiple_of(j * T, T)
        k = k16_ref[pl.ds(r0, nt * T), :]
        s = lax.dot_general(k, qm_ref[...], (((1,), (1,)), ((), ())), preferred_element_type=F32)
        if bidx is not None:
            bias = bias_ref[bidx, 0]
            s = jnp.concatenate([s[:, 0:T] + bias, s[:, T:2 * T] + bias], axis=1)
        s3 = s.reshape(nt * T // S8, S8, 2 * T)
        m_old = m_ref[...]
        m_new = jnp.maximum(m_old, _sublane_all(jnp.max(s3, axis=0), jnp.maximum))
        alpha = jnp.exp2(m_old - m_new)
        p3 = jnp.exp2(s3 - m_new[None])
        l_ref[...] = alpha * l_ref[...] + jnp.sum(p3, axis=0)
        p = p3.reshape(nt * T, 2 * T).astype(BF16)
        pv = jnp.dot(vt_ref[j], p[0:T], preferred_element_type=F32)
        for u in range(1, nt):
            pv = pv + jnp.dot(vt_ref[j + u], p[u * T:(u + 1) * T], preferred_element_type=F32)
        acc_ref[...] = alpha[None] * acc_ref[...] + pv.reshape(dv // S8, S8, 2 * T)
        m_ref[...] = m_new

    tile(0, jnp.where(i == 0, B_00, jnp.where(i == 1, B_10, B_X0)))

    n_far = jnp.maximum(i - 2, 0)
    n_quad = n_far // 4

    def far(jj, carry):
        tile(1 + 4 * jj, None, nt=4)
        return carry

    lax.fori_loop(0, n_quad, far, 0)

    @pl.when(n_far % 4 >= 2)
    def _():
        tile(1 + 4 * n_quad, None, nt=2)

    @pl.when(n_far % 2 == 1)
    def _():
        tile(i - 2, None)

    @pl.when(i >= 2)
    def _():
        tile(i - 1, B_SUB)

    @pl.when(i >= 1)
    def _():
        tile(i, B_DIAG)

    lam = lam_ref[0]
    on = acc_ref[...] / _sublane_all(l_ref[...], jnp.add)[None]
    o3 = on[:, :, 0:T] - lam * on[:, :, T:2 * T]
    ms = _sublane_all(jnp.sum(o3 * o3, axis=0), jnp.add) * (1.0 / dv)
    y = (o3 * lax.rsqrt(ms + EPS)[None]).reshape(dv, T).T
    o_ref[...] = (y * sub_ref[...] * out_scale).astype(o_ref.dtype)


def _attn(proj, bank, lam, subln, *, B, LB, H, dv, lam_init):
    T = SEQ_TILE
    nQ = LB // T
    R = proj.shape[0]
    dk = dv // 2
    qc, kc, vc = 5 * H, 6 * H, 7 * H
    return pl.pallas_call(
        functools.partial(_attn_kernel, T=T, LB=LB, dk=dk, dv=dv, scale=dk ** -0.5, out_scale=1.0 - lam_init),
        out_shape=jax.ShapeDtypeStruct((R, H * dv), BF16),
        grid=(B, H, nQ),
        in_specs=[pl.BlockSpec(memory_space=pltpu.SMEM),
                  pl.BlockSpec((T, dv), lambda b, h, i: (b * nQ + i, qc + h)),
                  pl.BlockSpec((LB, dv), lambda b, h, i: (b, kc + h)),
                  pl.BlockSpec((LB, dv), lambda b, h, i: (b, vc + h)),
                  pl.BlockSpec((6, 1, T, T), lambda b, h, i: (0, h, 0, 0)),
                  pl.BlockSpec((1, dv), lambda b, h, i: (0, 0))],
        out_specs=pl.BlockSpec((T, dv), lambda b, h, i: (b * nQ + i, h)),
        scratch_shapes=[pltpu.VMEM((2 * T, dv), BF16), pltpu.VMEM((LB, dv), BF16), pltpu.VMEM((nQ, dv, T), BF16),
                        pltpu.VMEM((8, 2 * T), F32), pltpu.VMEM((8, 2 * T), F32),
                        pltpu.VMEM((dv // 8, 8, 2 * T), F32)],
        compiler_params=_cparams(("parallel", "parallel", "arbitrary")),
        name="attn",
    )(lam, proj, proj, proj, jnp.swapaxes(bank, -1, -2), subln.reshape(1, dv))


def _log_forget(zf, log_lb, log1m_lb):
    ls = jnp.minimum(zf, 0.0) - jnp.log1p(jnp.exp(-jnp.abs(zf)))
    b = log1m_lb + ls
    hi = jnp.maximum(log_lb, b)
    lo = jnp.minimum(log_lb, b)
    return hi + jnp.log1p(jnp.exp(lo - hi))


def _split3(x):
    hi = x.astype(BF16)
    r = x - hi.astype(F32)
    mid = r.astype(BF16)
    lo = (r - mid.astype(F32)).astype(BF16)
    return hi, mid, lo


def _hgrn_kernel(q_ref, f_ref, i_ref, g_ref, loglb_ref, log1m_ref, omlb_ref, gn_ref,
                 o_ref, s_ref, st_ref, *, T, n_pad, H, dk):
    t = pl.program_id(1)

    @pl.when(t == 0)
    def _():
        st_ref[...] = jnp.zeros(st_ref.shape, F32)

    U = 8
    rr = lax.broadcasted_iota(jnp.int32, (SUB, SUB), 0)
    cc = lax.broadcasted_iota(jnp.int32, (SUB, SUB), 1)
    tri = jnp.where(cc <= rr, 1.0, 0.0).astype(BF16)
    row = lax.broadcasted_iota(jnp.int32, (SUB, 1), 0)
    urow = lax.broadcasted_iota(jnp.int32, (U, 1), 0)

    def chunk(c, carry):
        r0 = pl.multiple_of(c * SUB, SUB)
        zf = f_ref[pl.ds(r0, SUB), :]
        logf = _log_forget(zf, loglb_ref[...], log1m_ref[...])
        hi, mid, lo = _split3(logf)
        bcum = (jnp.dot(tri, hi, preferred_element_type=F32) + jnp.dot(tri, mid, preferred_element_type=F32)
                + jnp.dot(tri, lo, preferred_element_type=F32))
        kd = omlb_ref[...] * _sigmoid(-zf)
        zq = q_ref[pl.ds(r0, SUB), :]
        qd = zq * _sigmoid(zq)
        seq_row = t * T + r0 + row
        vd = jnp.where(seq_row >= n_pad, i_ref[pl.ds(r0, SUB), :], 0.0)
        zg = g_ref[pl.ds(r0, SUB), :]
        og = zg * _sigmoid(zg)
        blast, bmid = bcum[SUB - 1:SUB, :], bcum[U - 1:U, :]
        qs = (qd * jnp.exp(bcum)).astype(BF16)
        kt = (kd * jnp.exp(blast - bcum)).astype(BF16)
        q1 = (qd[U:SUB] * jnp.exp(bcum[U:SUB] - bmid)).astype(BF16)
        k0 = (kd[0:U] * jnp.exp(bmid - bcum[0:U])).astype(BF16)
        v16 = vd.astype(BF16)
        glast = jnp.exp(blast)
        o_state, a10 = [], []
        for h in range(H):
            sl = slice(h * dk, (h + 1) * dk)
            st = st_ref[h]
            o_state.append(lax.dot_general(qs[:, sl], st.astype(BF16), (((1,), (1,)), ((), ())),
                                           preferred_element_type=F32))
            a10.append(lax.dot_general(q1[:, sl], k0[:, sl], (((1,), (1,)), ((), ())),
                                       preferred_element_type=F32))
            kv = lax.dot_general(v16[:, sl], kt[:, sl], (((0,), (0,)), ((), ())), preferred_element_type=F32)
            st_ref[h] = st * glast[:, sl] + kv
        intra = []
        for h in range(H):
            sl = slice(h * dk, (h + 1) * dk)
            units = []
            for u in range(SUB // U):
                us = slice(u * U, (u + 1) * U)
                bu, qu, ku, vu = bcum[us, sl], qd[us, sl], kd[us, sl], vd[us, sl]
                a = None
                for d in range(U):
                    if d == 0:
                        kr, br, vr = ku, bu, vu
                    else:
                        kr, br, vr = pltpu.roll(ku, d, 0), pltpu.roll(bu, d, 0), pltpu.roll(vu, d, 0)
                    w = jnp.where(urow >= d, qu * kr * jnp.exp(bu - br), 0.0)
                    term = jnp.sum(w, axis=-1, keepdims=True) * vr
                    a = term if a is None else a + term
                units.append(a)
            intra.append(jnp.concatenate(units, axis=0))
        for h in range(H):
            sl = slice(h * dk, (h + 1) * dk)
            cross = jnp.dot(a10[h].astype(BF16), v16[0:U, sl], preferred_element_type=F32)
            o = o_state[h] + intra[h] + jnp.concatenate([jnp.zeros((U, dk), F32), cross], axis=0)
            o_ref[pl.ds(r0, SUB), sl] = (_rms(o, gn_ref[...]) * og[:, sl]).astype(o_ref.dtype)
        return carry

    lax.fori_loop(0, T // SUB, chunk, 0)

    @pl.when(t == pl.num_programs(1) - 1)
    def _():
        for h in range(H):
            s_ref[0, h] = st_ref[h].T


def _hgrn(proj, log_lb, log1m_lb, om_lb, g_norm, *, B, LB, n_pad, H, dk):
    T = SEQ_TILE
    nT = LB // T
    R = proj.shape[0]
    W = H * dk

    def seg(c):
        return pl.BlockSpec((T, W), lambda b, t: (b * nT + t, c))

    def par(n):
        return pl.BlockSpec((1, n), lambda b, t: (0, 0))

    return pl.pallas_call(
        functools.partial(_hgrn_kernel, T=T, n_pad=n_pad, H=H, dk=dk),
        out_shape=(jax.ShapeDtypeStruct((R, W), BF16), jax.ShapeDtypeStruct((B, H, dk, dk), F32)),
        grid=(B, nT),
        in_specs=[seg(8), seg(9), seg(10), seg(11), par(W), par(W), par(W), par(dk)],
        out_specs=(pl.BlockSpec((T, W), lambda b, t: (b * nT + t, 0)),
                   pl.BlockSpec((1, H, dk, dk), lambda b, t: (b, 0, 0, 0))),
        scratch_shapes=[pltpu.VMEM((H, dk, dk), F32)],
        compiler_params=_cparams(("parallel", "arbitrary")),
        name="hgrn",
    )(proj, proj, proj, proj, log_lb, log1m_lb, om_lb, g_norm.reshape(1, dk))


def _sample_mix_kernel(p_ref, sa_ref, sb_ref, sh_ref, dwa_ref, dwab_ref, lng_ref, lnb_ref, cb_ref,
                       lb_ref, omlb_ref, gn_ref, pa_in, pb_in, pd_in,
                       pa_ref, pb_ref, pd_ref, nsa_ref, nsb_ref, nsh_ref, o_scr, *, NB, W, H, dk, wa, wb):
    del pa_in, pb_in, pd_in

    def seg(c):
        return p_ref[:, c * W:(c + 1) * W]

    pad = jnp.zeros((SUB - NB, W), F32)

    glu = seg(0) * _sigmoid(seg(1))
    rows = []
    for b in range(NB):
        cv = jnp.sum(sa_ref[b] * dwa_ref[0:wa - 1, :], axis=0, keepdims=True)
        rows.append(cv + dwa_ref[wa - 1:wa, :] * glu[b:b + 1, :])
        nsa_ref[b, 0:wa - 2, :] = sa_ref[b, 1:wa - 1, :]
        nsa_ref[b, wa - 2:wa - 1, :] = glu[b:b + 1, :]
    acc = jnp.concatenate(rows, axis=0) + dwab_ref[...]
    mu = jnp.mean(acc, axis=-1, keepdims=True)
    cen = acc - mu
    var = jnp.mean(cen * cen, axis=-1, keepdims=True)
    y = cen * lax.rsqrt(var + EPS) * lng_ref[...] + lnb_ref[...]
    pa_ref[...] = jnp.concatenate([y * _sigmoid(y), pad], axis=0).astype(pa_ref.dtype)

    u = seg(3) * seg(4)
    rows = []
    for b in range(NB):
        cv = jnp.sum(sb_ref[b] * cb_ref[0:wb - 1, :], axis=0, keepdims=True)
        rows.append(cv + cb_ref[wb - 1:wb, :] * u[b:b + 1, :])
        if wb > 2:
            nsb_ref[b, 0:wb - 2, :] = sb_ref[b, 1:wb - 1, :]
        nsb_ref[b, wb - 2:wb - 1, :] = u[b:b + 1, :]
    pb_ref[...] = jnp.concatenate([seg(2) * jnp.concatenate(rows, axis=0), pad], axis=0).astype(pb_ref.dtype)

    zf = seg(9)
    sg = _sigmoid(zf)
    fg = lb_ref[...] + omlb_ref[...] * sg
    kd = omlb_ref[...] * _sigmoid(-zf)
    zq = seg(8)
    qd = zq * _sigmoid(zq)
    vd = seg(10)
    eye = lax.broadcasted_iota(jnp.int32, (dk, dk), 0) == lax.broadcasted_iota(jnp.int32, (dk, dk), 1)

    def col(x):
        return jnp.sum(jnp.where(eye, x, 0.0), axis=1, keepdims=True)

    for b in range(NB):
        for h in range(H):
            sl = slice(h * dk, (h + 1) * dk)
            s_new = col(fg[b:b + 1, sl]) * sh_ref[b, h] + col(kd[b:b + 1, sl]) * vd[b:b + 1, sl]
            nsh_ref[b, h] = s_new
            o_scr[b:b + 1, sl] = jnp.sum(col(qd[b:b + 1, sl]) * s_new, axis=0, keepdims=True)
    zg = seg(11)
    og = zg * _sigmoid(zg)
    o = o_scr[...]
    outs = [_rms(o[:, h * dk:(h + 1) * dk], gn_ref[...]) for h in range(H)]
    pd_ref[...] = jnp.concatenate([jnp.concatenate(outs, axis=1) * og, pad], axis=0).astype(pd_ref.dtype)


def _sample_mix(proj, sa, sb, sh, dw_a, dw_a_bias, ln_g, ln_b, conv_b, lb, om_lb, g_norm, pre_a, pre_b, pre_d):
    NB, wa1, W = sa.shape
    wb1 = sb.shape[1]
    H, dk = sh.shape[1], sh.shape[2]
    R = proj.shape[0]

    def full(shape):
        return pl.BlockSpec(shape, lambda i: (0,) * len(shape))

    rows = pl.BlockSpec((SUB, W), lambda i: (0, 0))
    return pl.pallas_call(
        functools.partial(_sample_mix_kernel, NB=NB, W=W, H=H, dk=dk, wa=wa1 + 1, wb=wb1 + 1),
        out_shape=(jax.ShapeDtypeStruct((R, W), BF16), jax.ShapeDtypeStruct((R, W), BF16),
                   jax.ShapeDtypeStruct((R, W), BF16), jax.ShapeDtypeStruct(sa.shape, F32),
                   jax.ShapeDtypeStruct(sb.shape, F32), jax.ShapeDtypeStruct(sh.shape, F32)),
        grid=(1,),
        in_specs=[pl.BlockSpec((NB, N_SEG * W), lambda i: (0, 0)), full(sa.shape), full(sb.shape), full(sh.shape),
                  full((wa1 + 1, W)), full((1, W)), full((1, W)), full((1, W)), full((wb1 + 1, W)),
                  full((1, W)), full((1, W)), full((1, dk)),
                  pl.BlockSpec(memory_space=pl.ANY), pl.BlockSpec(memory_space=pl.ANY),
                  pl.BlockSpec(memory_space=pl.ANY)],
        out_specs=(rows, rows, rows, full(sa.shape), full(sb.shape), full(sh.shape)),
        scratch_shapes=[pltpu.VMEM((NB, W), F32)],
        input_output_aliases={12: 0, 13: 1, 14: 2},
        compiler_params=_cparams(("arbitrary",)),
        name="sample_mix",
    )(proj, sa, sb, sh, dw_a, dw_a_bias.reshape(1, W), ln_g.reshape(1, W), ln_b.reshape(1, W), conv_b,
      lb, om_lb, g_norm.reshape(1, dk), pre_a, pre_b, pre_d)


def _decode_kernel(pt_ref, lam_ref, q_ref, kn_ref, vn_ref, bias_ref, biasn_ref, sub_ref, pc_in, *rest,
                   G, P, H, dv, scale, out_scale, NB):
    del pt_ref, pc_in
    k_refs, v_refs = rest[:G], rest[G:2 * G]
    o_ref, m_ref, l_ref, acc_ref, o_scr = rest[2 * G:]
    b = pl.program_id(0)
    s = pl.program_id(1)
    dk = dv // 2
    HM = 2 * H
    W = H * dv
    C = P * H
    last = s == pl.num_programs(1) - 1

    @pl.when(s == 0)
    def _():
        m_ref[...] = jnp.full(m_ref.shape, NEG, F32)
        l_ref[...] = jnp.zeros(l_ref.shape, F32)
        acc_ref[...] = jnp.zeros(acc_ref.shape, F32)

    def per_map_rows(row):
        return jnp.concatenate([row[:, h * dv:(h + 1) * dv] for h in range(H) for _ in range(2)], axis=0)

    rr = lax.broadcasted_iota(jnp.int32, (HM, dv), 0)
    ll = lax.broadcasted_iota(jnp.int32, (HM, dv), 1)
    own_map = (ll >= dk) == (jnp.bitwise_and(rr, 1) == 1)
    qall = jnp.where(own_map, per_map_rows(q_ref[pl.ds(b, 1), :]) * (scale * LOG2E), 0.0)
    qall16 = qall.astype(BF16)
    col_head = jnp.bitwise_and(lax.broadcasted_iota(jnp.int32, (HM, C), 1), H - 1)
    row_head = lax.shift_right_logical(lax.broadcasted_iota(jnp.int32, (HM, C), 0), 1)
    own_head = col_head == row_head

    blocks = []
    for g in range(G):
        kg = k_refs[g][...].reshape(C, dv).astype(BF16)
        sg = lax.dot_general(qall16, kg, (((1,), (1,)), ((), ())), preferred_element_type=F32)
        if g == G - 1:
            sg = sg + jnp.where(last, bias_ref[...], 0.0)
        blocks.append(jnp.where(own_head, sg, NEG))
    sc = jnp.concatenate(blocks, axis=1)
    m_old = m_ref[...]
    m_new = jnp.maximum(m_old, jnp.max(sc, axis=-1, keepdims=True))
    alpha = jnp.exp2(m_old - m_new)
    p = jnp.exp2(sc - m_new)
    l_ref[...] = alpha * l_ref[...] + jnp.sum(p, axis=-1, keepdims=True)
    pv = None
    for g in range(G):
        vg = v_refs[g][...].reshape(C, dv).astype(BF16)
        y = jnp.dot(p[:, g * C:(g + 1) * C].astype(BF16), vg, preferred_element_type=F32)
        pv = y if pv is None else pv + y
    acc_ref[...] = alpha * acc_ref[...] + pv
    m_ref[...] = m_new

    @pl.when(last)
    def _():
        kn = per_map_rows(kn_ref[pl.ds(b, 1), :])
        vn = per_map_rows(vn_ref[pl.ds(b, 1), :])
        sn = jnp.sum(qall * kn, axis=-1, keepdims=True) + biasn_ref[...]
        m1 = m_ref[...]
        m2 = jnp.maximum(m1, sn)
        a2 = jnp.exp2(m1 - m2)
        pn = jnp.exp2(sn - m2)
        o = (a2 * acc_ref[...] + pn * vn) / (a2 * l_ref[...] + pn)
        lam = lam_ref[0]
        outs = []
        for h in range(H):
            oh = o[2 * h:2 * h + 1, :] - lam * o[2 * h + 1:2 * h + 2, :]
            outs.append(_rms(oh, sub_ref[...]) * out_scale)
        o_scr[pl.ds(b, 1), :] = jnp.concatenate(outs, axis=1)

    @pl.when((s == pl.num_programs(1) - 1) & (b == NB - 1))
    def _():
        o_ref[...] = jnp.concatenate([o_scr[...], jnp.zeros((SUB - NB, W), F32)], axis=0).astype(o_ref.dtype)


def _decode_attn(proj, cache_k, cache_v, page_table, layer, bias_past, bias_new, lam, subln, pre_c,
                 *, H, dv, lam_init):
    NB, n_pages = page_table.shape
    P = cache_k.shape[2]
    G = PAGES_PER_STEP
    assert n_pages % G == 0 and P >= MAX_DISTANCE and H & (H - 1) == 0
    W = H * dv
    HM = 2 * H

    def page(g):
        return pl.BlockSpec((None, None, P, H, dv), lambda b, s, pt: (pt[b, s * G + g], layer, 0, 0, 0))

    def rows(c):
        return pl.BlockSpec((NB, W), lambda b, s, pt: (0, c))

    grid_spec = pltpu.PrefetchScalarGridSpec(
        num_scalar_prefetch=1,
        grid=(NB, n_pages // G),
        in_specs=[pl.BlockSpec(memory_space=pltpu.SMEM), rows(5), rows(6), rows(7),
                  pl.BlockSpec((HM, P * H), lambda b, s, pt: (0, 0)),
                  pl.BlockSpec((HM, 1), lambda b, s, pt: (0, 0)),
                  pl.BlockSpec((1, dv), lambda b, s, pt: (0, 0)),
                  pl.BlockSpec(memory_space=pl.ANY)]
        + [page(g) for g in range(G)] + [page(g) for g in range(G)],
        out_specs=pl.BlockSpec((SUB, W), lambda b, s, pt: (0, 0)),
        scratch_shapes=[pltpu.VMEM((HM, 1), F32), pltpu.VMEM((HM, 1), F32), pltpu.VMEM((HM, dv), F32),
                        pltpu.VMEM((NB, W), F32)],
    )
    return pl.pallas_call(
        functools.partial(_decode_kernel, G=G, P=P, H=H, dv=dv, scale=(dv // 2) ** -0.5,
                          out_scale=1.0 - lam_init, NB=NB),
        out_shape=jax.ShapeDtypeStruct(pre_c.shape, pre_c.dtype),
        grid_spec=grid_spec,
        input_output_aliases={8: 0},
        compiler_params=_cparams(("arbitrary", "arbitrary")),
        name="decode_attn",
    )(page_table, lam, proj, proj, proj, bias_past, bias_new, subln.reshape(1, dv), pre_c,
      *([cache_k] * G), *([cache_v] * G))


def kernel(x_prompt, x_sample, cache_k, cache_v, state_conv_a, state_conv_b, state_hgrn, page_table, meta_tokens, rel_bias_table, hgrn_lower_bound, norm1, w_in, dw_a, dw_a_bias, ln_a_g, ln_a_b, w_a_out, conv_b, w_b_out, lam_q1, lam_k1, lam_q2, lam_k2, subln, w_c_out, g_norm_d, w_d_out, w_gate, b_gate, w_o, norm2, w_up, w_down, final_norm):
    B, L, D = x_prompt.shape
    NB = x_sample.shape[0]
    assert x_sample.shape[1] == 1
    depth = w_in.shape[0]
    n_meta = meta_tokens.shape[0]
    W = state_conv_a.shape[-1]
    H, dv = cache_v.shape[3], cache_v.shape[4]
    Hr, dkr = state_hgrn.shape[2], state_hgrn.shape[3]
    n_buckets = rel_bias_table.shape[0]
    past_len = page_table.shape[1] * cache_k.shape[2]
    assert w_in.shape[2] == N_SEG * W and H * dv == W and Hr * dkr == W and state_conv_b.shape[-1] == W
    assert cache_k.shape[4] == dv and state_hgrn.shape[4] == dkr and dv == LANES and dkr == LANES

    T = SEQ_TILE
    Lt = n_meta + L
    LB = -(-(Lt + NB) // T) * T
    n_pad = LB - Lt
    R = B * LB
    tm = ROW_TILE if R % ROW_TILE == 0 else T
    assert n_pad <= T and NB <= SUB <= n_pad and R % tm == 0

    meta = jnp.broadcast_to(meta_tokens[None].astype(F32), (B, n_meta, D))
    h = jnp.concatenate([jnp.zeros((B, n_pad, D), F32), meta, x_prompt], axis=1)
    h = h.at[0, :NB].set(x_sample[:, 0]).reshape(R, D)

    lbs = jnp.cumsum(jax.nn.softmax(hgrn_lower_bound.astype(F32), axis=0), axis=0)
    lbs = lbs - lbs[0:1]
    bank = _bias_bank(rel_bias_table, T, n_pad)
    pos_s = jnp.full((1,), past_len, jnp.int32)
    P = cache_k.shape[2]
    kpos_near = jnp.arange(past_len - P, past_len + 1)
    bias_dec = jnp.repeat(_rel_bias(pos_s, kpos_near, rel_bias_table, n_buckets)[:, 0, :], 2, axis=0) * LOG2E
    bias_past, bias_new = jnp.repeat(bias_dec[:, :P], H, axis=1), bias_dec[:, P:]

    w_in16, w_gate16, w_o16 = w_in.astype(BF16), w_gate.astype(BF16), w_o.astype(BF16)
    w_up16, w_down16 = w_up.astype(BF16), w_down.astype(BF16)
    w_branch16 = jnp.stack([w_a_out, w_b_out, w_c_out, w_d_out], axis=1).astype(BF16)

    k_p, v_p, ca_p, cb_p, s_p = [], [], [], [], []
    k_s, v_s, ca_s, cb_s, s_s = [], [], [], [], []
    for l in range(depth):
        lam_init = 0.8 - 0.6 * math.exp(-0.3 * l)
        lam = (jnp.exp(jnp.sum(lam_q1[l].astype(F32) * lam_k1[l].astype(F32)))
               - jnp.exp(jnp.sum(lam_q2[l].astype(F32) * lam_k2[l].astype(F32))) + lam_init).reshape(1)
        lb = lbs[l].reshape(1, W)
        log_lb, log1m_lb, om_lb = jnp.log(lb), jnp.log1p(-lb), 1.0 - lb

        proj, kv = _matmul(h, w_in16, l, tm=tm, tn=COL_TILE, out_dtype=F32, norm_g=norm1[l],
                           heads=(6, 2, H, dv), name="proj")
        gates = _matmul(h, w_gate16, l, tm=tm, tn=COL_TILE, out_dtype=BF16, norm_g=norm1[l],
                        bias=b_gate[l], act="sigmoid", name="gates")

        pre_a, pre_b, ca, cb = _conv(proj, dw_a[l], dw_a_bias[l], ln_a_g[l], ln_a_b[l], conv_b[l],
                                     B=B, LB=LB, n_pad=n_pad, W=W)
        pre_c = _attn(proj, bank, lam, subln[l], B=B, LB=LB, H=H, dv=dv, lam_init=lam_init)
        pre_d, s_fin = _hgrn(proj, log_lb, log1m_lb, om_lb, g_norm_d[l], B=B, LB=LB, n_pad=n_pad, H=Hr, dk=dkr)

        pre_a, pre_b, pre_d, nsa, nsb, nsh = _sample_mix(
            proj, state_conv_a[l], state_conv_b[l], state_hgrn[l], dw_a[l], dw_a_bias[l], ln_a_g[l], ln_a_b[l],
            conv_b[l], lb, om_lb, g_norm_d[l], pre_a, pre_b, pre_d)
        pre_c = _decode_attn(proj, cache_k, cache_v, page_table, l, bias_past, bias_new, lam, subln[l], pre_c,
                             H=H, dv=dv, lam_init=lam_init)

        merged = _merge((pre_a, pre_b, pre_c, pre_d), w_branch16, l, gates, tm=tm, tn=MERGE_COL_TILE)
        h = _matmul(merged, w_o16, l, tm=tm, tn=COL_TILE, out_dtype=F32, residual=h, name="w_o")
        last = l == depth - 1
        h = _mlp(h, norm2[l], w_up16, w_down16, l, final_norm if last else None, tm=tm, tf=FF_TILE)

        kv5 = kv.reshape(2, B, LB, H, dv)
        k_p.append(kv5[0, :, n_pad:])
        v_p.append(kv5[1, :, n_pad:])
        k_s.append(kv[0, :NB])
        v_s.append(kv[1, :NB])
        ca_p.append(ca); cb_p.append(cb); s_p.append(s_fin)
        ca_s.append(nsa); cb_s.append(nsb); s_s.append(nsh)

    y3 = h.reshape(B, LB, D)
    y_prompt = y3[:, n_pad + n_meta:]
    y_sample = y3[0, :NB].reshape(NB, 1, D)
    new_k_prompt = jnp.stack(k_p, axis=2)
    new_v_prompt = jnp.stack(v_p, axis=2)
    new_k_sample = jnp.stack(k_s, axis=1).reshape(NB, 1, depth, H, dv)
    new_v_sample = jnp.stack(v_s, axis=1).reshape(NB, 1, depth, H, dv)
    return (y_prompt, y_sample, new_k_prompt, new_v_prompt, jnp.stack(ca_p, 0), jnp.stack(cb_p, 0),
            jnp.stack(s_p, 0), new_k_sample, new_v_sample, jnp.stack(ca_s, 0), jnp.stack(cb_s, 0),
            jnp.stack(s_s, 0))
```

```python
import functools
import math

import jax
import jax.numpy as jnp
from jax import lax
from jax.experimental import pallas as pl
from jax.experimental.pallas import tpu as pltpu

F32 = jnp.float32
BF16 = jnp.bfloat16

EPS = 1e-6
MAX_DISTANCE = 128
NEG = -1e30
LOG2E = math.log2(math.e)

LANES = 128
SEQ_TILE = 384
ROW_TILE = 768
COL_TILE = 1024
MERGE_COL_TILE = 1024
FF_TILE = 1024
SUB = 16
PAGES_PER_STEP = 8
VMEM_LIMIT = 56 * 1024 * 1024
N_SEG = 12
N_BRANCH = 4


def _cparams(sem):
    return pltpu.CompilerParams(dimension_semantics=sem, vmem_limit_bytes=VMEM_LIMIT)


def _sigmoid(x):
    return 1.0 / (1.0 + jnp.exp(-x))


def _rms(x, g):
    return x * lax.rsqrt(jnp.mean(x * x, axis=-1, keepdims=True) + EPS) * g


def _mm_kernel(*refs, has_norm, has_bias, act, has_res, heads):
    it = iter(refs)
    x_ref = next(it)
    g_ref = next(it) if has_norm else None
    w_ref = next(it)
    b_ref = next(it) if has_bias else None
    r_ref = next(it) if has_res else None
    o_ref = next(it)
    hv_ref = next(it) if heads else None
    if has_norm:
        xn_ref = next(it)

        @pl.when(pl.program_id(1) == 0)
        def _():
            xn_ref[...] = _rms(x_ref[...], g_ref[...]).astype(BF16)

        lhs = xn_ref[...]
    else:
        lhs = x_ref[...]
    y = jnp.dot(lhs, w_ref[...], preferred_element_type=F32)
    if has_bias:
        y = y + b_ref[...]
    if act == "sigmoid":
        y = _sigmoid(y)
    if has_res:
        y = y + r_ref[...]
    o_ref[...] = y.astype(o_ref.dtype)
    if heads:
        j0, n, H, dv = heads
        j = pl.program_id(1)

        @pl.when((j >= j0) & (j < j0 + n))
        def _():
            for h in range(H):
                hv_ref[:, h, :] = y[:, h * dv:(h + 1) * dv]


def _matmul(x, w, layer, *, tm, tn, out_dtype, norm_g=None, bias=None, act=None, residual=None, heads=None, name):
    M, K = x.shape
    N = w.shape[2]
    tn = min(tn, N)
    if heads is not None:
        tn = heads[2] * heads[3]
    assert M % tm == 0 and N % tn == 0
    args = [x]
    specs = [pl.BlockSpec((tm, K), lambda i, j: (i, 0))]
    scratch = []
    if norm_g is not None:
        args.append(norm_g.reshape(1, K))
        specs.append(pl.BlockSpec((1, K), lambda i, j: (0, 0)))
        scratch.append(pltpu.VMEM((tm, K), BF16))
    args.append(w)
    specs.append(pl.BlockSpec((None, K, tn), lambda i, j: (layer, 0, j)))
    if bias is not None:
        args.append(bias.reshape(1, N))
        specs.append(pl.BlockSpec((1, tn), lambda i, j: (0, j)))
    if residual is not None:
        args.append(residual)
        specs.append(pl.BlockSpec((tm, tn), lambda i, j: (i, j)))
    kern = functools.partial(_mm_kernel, has_norm=norm_g is not None, has_bias=bias is not None,
                             act=act, has_res=residual is not None, heads=heads)
    out_shape = jax.ShapeDtypeStruct((M, N), out_dtype)
    out_specs = pl.BlockSpec((tm, tn), lambda i, j: (i, j))
    if heads is not None:
        j0, n, H, dv = heads
        out_shape = (out_shape, jax.ShapeDtypeStruct((n, M, H, dv), out_dtype))
        out_specs = (out_specs, pl.BlockSpec((None, tm, H, dv), lambda i, j: (jnp.clip(j - j0, 0, n - 1), i, 0, 0)))
    return pl.pallas_call(
        kern,
        out_shape=out_shape,
        grid=(M // tm, N // tn),
        in_specs=specs,
        out_specs=out_specs,
        scratch_shapes=scratch,
        compiler_params=_cparams(("parallel", "arbitrary")),
        name=name,
    )(*args)


def _merge_kernel(pa_ref, pb_ref, pc_ref, pd_ref, w_ref, g0_ref, g1_ref, g2_ref, g3_ref, o_ref):
    acc = None
    for k, (p_ref, g_ref) in enumerate(((pa_ref, g0_ref), (pb_ref, g1_ref), (pc_ref, g2_ref), (pd_ref, g3_ref))):
        y = jnp.dot(p_ref[...], w_ref[k], preferred_element_type=F32) * g_ref[...].astype(F32)
        acc = y if acc is None else acc + y
    o_ref[...] = acc.astype(o_ref.dtype)


def _merge(pres, w_branch, layer, gates, *, tm, tn):
    R, W = pres[0].shape
    D = w_branch.shape[3]
    tn = min(tn, D)
    n_col = D // tn
    pre_spec = pl.BlockSpec((tm, W), lambda i, j: (i, 0))
    gate_specs = [pl.BlockSpec((tm, tn), functools.partial(lambda i, j, b: (i, b * n_col + j), b=b))
                  for b in range(N_BRANCH)]
    return pl.pallas_call(
        _merge_kernel,
        out_shape=jax.ShapeDtypeStruct((R, D), BF16),
        grid=(R // tm, n_col),
        in_specs=([pre_spec] * N_BRANCH
                  + [pl.BlockSpec((None, N_BRANCH, W, tn), lambda i, j: (layer, 0, 0, j))] + gate_specs),
        out_specs=pl.BlockSpec((tm, tn), lambda i, j: (i, j)),
        compiler_params=_cparams(("parallel", "arbitrary")),
        name="merge",
    )(*pres, w_branch, gates, gates, gates, gates)


def _mlp_kernel(*refs, final):
    if final:
        h_ref, g_ref, wu_ref, wd_ref, fn_ref, o_ref, hn_ref = refs
    else:
        h_ref, g_ref, wu_ref, wd_ref, o_ref, hn_ref = refs
    f = pl.program_id(1)

    @pl.when(f == 0)
    def _():
        hn_ref[...] = _rms(h_ref[...], g_ref[...]).astype(BF16)
        o_ref[...] = h_ref[...]

    u = jnp.dot(hn_ref[...], wu_ref[...], preferred_element_type=F32)
    u = jnp.square(jnp.maximum(u, 0.0))
    o_ref[...] += jnp.dot(u.astype(BF16), wd_ref[...], preferred_element_type=F32)

    if final:
        @pl.when(f == pl.num_programs(1) - 1)
        def _():
            o_ref[...] = _rms(o_ref[...], fn_ref[...])


def _mlp(h, norm_g, w_up, w_down, layer, final_g, *, tm, tf):
    R, D = h.shape
    FF = w_up.shape[2]
    final = final_g is not None
    args = [h, norm_g.reshape(1, D), w_up, w_down]
    specs = [pl.BlockSpec((tm, D), lambda i, f: (i, 0)),
             pl.BlockSpec((1, D), lambda i, f: (0, 0)),
             pl.BlockSpec((None, D, tf), lambda i, f: (layer, 0, f)),
             pl.BlockSpec((None, tf, D), lambda i, f: (layer, f, 0))]
    if final:
        args.append(final_g.reshape(1, D))
        specs.append(pl.BlockSpec((1, D), lambda i, f: (0, 0)))
    return pl.pallas_call(
        functools.partial(_mlp_kernel, final=final),
        out_shape=jax.ShapeDtypeStruct((R, D), F32),
        grid=(R // tm, FF // tf),
        in_specs=specs,
        out_specs=pl.BlockSpec((tm, D), lambda i, f: (i, 0)),
        scratch_shapes=[pltpu.VMEM((tm, D), BF16)],
        compiler_params=_cparams(("parallel", "arbitrary")),
        name="mlp",
    )(*args)


def _conv_kernel(au_ref, ag_ref, bb_ref, bc_ref, bh_ref, dwa_ref, dwab_ref, lng_ref, lnb_ref, cb_ref,
                 pa_ref, pb_ref, sa_ref, sb_ref, xa_ref, xb_ref, *, T, n_pad, wa, wb):
    t = pl.program_id(1)
    ha = 32
    hb = 8

    @pl.when(t > 0)
    def _():
        xa_ref[0:ha, :] = xa_ref[T:T + ha, :]
        xb_ref[0:hb, :] = xb_ref[T:T + hb, :]

    xa_ref[ha:ha + T, :] = au_ref[...] * _sigmoid(ag_ref[...])
    xb_ref[hb:hb + T, :] = bc_ref[...] * bh_ref[...]

    @pl.when(t == 0)
    def _():
        xa_ref[0:ha + n_pad, :] = jnp.zeros((ha + n_pad, xa_ref.shape[1]), F32)
        xb_ref[0:hb + n_pad, :] = jnp.zeros((hb + n_pad, xb_ref.shape[1]), F32)

    def chunk(c, carry):
        r0 = pl.multiple_of(c * SUB, SUB)
        cols = []
        for g in range(xa_ref.shape[1] // LANES):
            ls = slice(g * LANES, (g + 1) * LANES)
            win = xa_ref[pl.ds(r0, ha + SUB), ls]
            a = None
            for rho in range(8):
                taps = [j for j in range(wa) if (ha - (wa - 1) + j) % 8 == rho]
                if not taps:
                    continue
                rot = win if rho == 0 else pltpu.roll(win, ha + SUB - rho, 0)
                for j in taps:
                    o = ha - (wa - 1) + j - rho
                    term = dwa_ref[j:j + 1, ls] * rot[o:o + SUB, :]
                    a = term if a is None else a + term
            cols.append(a)
        acc = jnp.concatenate(cols, axis=1) + dwab_ref[...]
        mu = jnp.mean(acc, axis=-1, keepdims=True)
        cen = acc - mu
        var = jnp.mean(cen * cen, axis=-1, keepdims=True)
        y = cen * lax.rsqrt(var + EPS) * lng_ref[...] + lnb_ref[...]
        pa_ref[pl.ds(r0, SUB), :] = (y * _sigmoid(y)).astype(pa_ref.dtype)
        winb = xb_ref[pl.ds(r0, hb + SUB), :]
        cv = None
        for j in range(wb):
            o = hb - (wb - 1) + j
            term = cb_ref[j:j + 1, :] * winb[o:o + SUB, :]
            cv = term if cv is None else cv + term
        pb_ref[pl.ds(r0, SUB), :] = (bb_ref[pl.ds(r0, SUB), :] * cv).astype(pb_ref.dtype)
        return carry

    lax.fori_loop(0, T // SUB, chunk, 0)

    @pl.when(t == pl.num_programs(1) - 1)
    def _():
        sa_ref[0] = xa_ref[T + ha - (wa - 1):T + ha, :]
        sb_ref[0] = xb_ref[T + hb - (wb - 1):T + hb, :]


def _conv(proj, dw_a, dw_a_bias, ln_g, ln_b, conv_b, *, B, LB, n_pad, W):
    T = SEQ_TILE
    nT = LB // T
    wa, wb = dw_a.shape[0], conv_b.shape[0]
    assert wa - 1 <= 32 and wb - 1 <= 8 and n_pad <= T and n_pad % 8 == 0
    R = proj.shape[0]

    def seg(c):
        return pl.BlockSpec((T, W), lambda b, t: (b * nT + t, c))

    def par(n):
        return pl.BlockSpec((n, W), lambda b, t: (0, 0))

    row = pl.BlockSpec((T, W), lambda b, t: (b * nT + t, 0))
    return pl.pallas_call(
        functools.partial(_conv_kernel, T=T, n_pad=n_pad, wa=wa, wb=wb),
        out_shape=(jax.ShapeDtypeStruct((R, W), BF16), jax.ShapeDtypeStruct((R, W), BF16),
                   jax.ShapeDtypeStruct((B, wa - 1, W), F32), jax.ShapeDtypeStruct((B, wb - 1, W), F32)),
        grid=(B, nT),
        in_specs=[seg(0), seg(1), seg(2), seg(3), seg(4), par(wa), par(1), par(1), par(1), par(wb)],
        out_specs=(row, row,
                   pl.BlockSpec((1, wa - 1, W), lambda b, t: (b, 0, 0)),
                   pl.BlockSpec((1, wb - 1, W), lambda b, t: (b, 0, 0))),
        scratch_shapes=[pltpu.VMEM((T + 32, W), F32), pltpu.VMEM((T + 8, W), F32)],
        compiler_params=_cparams(("parallel", "arbitrary")),
        name="conv",
    )(proj, proj, proj, proj, proj, dw_a, dw_a_bias.reshape(1, W), ln_g.reshape(1, W), ln_b.reshape(1, W), conv_b)


B_DIAG, B_SUB, B_00, B_10, B_X0 = 1, 2, 3, 4, 5


def _rel_bias(qpos, kpos, table, n_buckets):
    n = jnp.maximum(qpos[:, None] - kpos[None, :], 0)
    max_exact = n_buckets // 2
    nf = jnp.maximum(n, 1).astype(F32)
    large = max_exact + (jnp.log(nf / max_exact) / math.log(MAX_DISTANCE / max_exact)
                         * (n_buckets - max_exact)).astype(jnp.int32)
    bucket = jnp.where(n < max_exact, n, jnp.minimum(large, n_buckets - 1))
    onehot = (bucket[:, :, None] == jnp.arange(n_buckets)[None, None, :]).astype(F32)
    bias = jnp.einsum("qkn,nh->hqk", onehot, table.astype(F32), precision=lax.Precision.HIGHEST)
    return bias - table[n_buckets - 1].astype(F32)[:, None, None]


def _bias_bank(table, T, n_pad):
    nb, H = table.shape
    r = jnp.arange(T)
    causal = (r[None, :] <= r[:, None])[None]
    kpad = (r < n_pad)[None, None, :]
    qval = (r >= n_pad)[None, :, None]
    zero = jnp.zeros((H, T, T), F32)
    diag = jnp.where(causal, _rel_bias(r, r, table, nb), NEG)
    sub = _rel_bias(r + T, r, table, nb)
    b00 = jnp.where(kpad & qval, NEG, diag)
    b10 = jnp.where(kpad, NEG, sub)
    bx0 = jnp.where(kpad, NEG, zero)
    return jnp.stack([zero, diag, sub, b00, b10, bx0], axis=0) * LOG2E


def _sublane_all(x, op):
    for sh in (4, 2, 1):
        x = op(x, pltpu.roll(x, sh, 0))
    return x


def _attn_kernel(lam_ref, q_ref, k_ref, v_ref, bias_ref, sub_ref, o_ref, qm_ref, k16_ref, vt_ref, m_ref, l_ref,
                 acc_ref, *, T, LB, dk, dv, scale, out_scale):
    i = pl.program_id(2)
    S8 = 8

    @pl.when(i == 0)
    def _():
        k16_ref[...] = k_ref[...].astype(BF16)
        for c in range(LB // LANES):
            r0 = c * LANES
            vt_ref[r0 // T, :, r0 % T:r0 % T + LANES] = v_ref[r0:r0 + LANES, :].T.astype(BF16)

    q = q_ref[...] * (scale * LOG2E)
    lane = lax.broadcasted_iota(jnp.int32, q.shape, 1)
    qm_ref[0:T, :] = jnp.where(lane < dk, q, 0.0).astype(BF16)
    qm_ref[T:2 * T, :] = jnp.where(lane >= dk, q, 0.0).astype(BF16)

    m_ref[...] = jnp.full(m_ref.shape, NEG, F32)
    l_ref[...] = jnp.zeros(l_ref.shape, F32)
    acc_ref[...] = jnp.zeros(acc_ref.shape, F32)

    def tile(j, bidx, nt=1):
        r0 = pl.multiple_of(j * T, T)
        k = k16_ref[pl.ds(r0, nt * T), :]
        s = lax.dot_general(k, qm_ref[...], (((1,), (1,)), ((), ())), preferred_element_type=F32)
        if bidx is not None:
            bias = bias_ref[bidx, 0]
            s = jnp.concatenate([s[:, 0:T] + bias, s[:, T:2 * T] + bias], axis=1)
        s3 = s.reshape(nt * T // S8, S8, 2 * T)
        m_old = m_ref[...]
        m_new = jnp.maximum(m_old, _sublane_all(jnp.max(s3, axis=0), jnp.maximum))
        alpha = jnp.exp2(m_old - m_new)
        p3 = jnp.exp2(s3 - m_new[None])
        l_ref[...] = alpha * l_ref[...] + jnp.sum(p3, axis=0)
        p = p3.reshape(nt * T, 2 * T).astype(BF16)
        pv = jnp.dot(vt_ref[j], p[0:T], preferred_element_type=F32)
        for u in range(1, nt):
            pv = pv + jnp.dot(vt_ref[j + u], p[u * T:(u + 1) * T], preferred_element_type=F32)
        acc_ref[...] = alpha[None] * acc_ref[...] + pv.reshape(dv // S8, S8, 2 * T)
        m_ref[...] = m_new

    tile(0, jnp.where(i == 0, B_00, jnp.where(i == 1, B_10, B_X0)))

    n_far = jnp.maximum(i - 2, 0)
    n_quad = n_far // 4

    def far(jj, carry):
        tile(1 + 4 * jj, None, nt=4)
        return carry

    lax.fori_loop(0, n_quad, far, 0)

    @pl.when(n_far % 4 >= 2)
    def _():
        tile(1 + 4 * n_quad, None, nt=2)

    @pl.when(n_far % 2 == 1)
    def _():
        tile(i - 2, None)

    @pl.when(i >= 2)
    def _():
        tile(i - 1, B_SUB)

    @pl.when(i >= 1)
    def _():
        tile(i, B_DIAG)

    lam = lam_ref[0]
    on = acc_ref[...] / _sublane_all(l_ref[...], jnp.add)[None]
    o3 = on[:, :, 0:T] - lam * on[:, :, T:2 * T]
    ms = _sublane_all(jnp.sum(o3 * o3, axis=0), jnp.add) * (1.0 / dv)
    y = (o3 * lax.rsqrt(ms + EPS)[None]).reshape(dv, T).T
    o_ref[...] = (y * sub_ref[...] * out_scale).astype(o_ref.dtype)


def _attn(proj, bank, lam, subln, *, B, LB, H, dv, lam_init):
    T = SEQ_TILE
    nQ = LB // T
    R = proj.shape[0]
    dk = dv // 2
    qc, kc, vc = 5 * H, 6 * H, 7 * H
    return pl.pallas_call(
        functools.partial(_attn_kernel, T=T, LB=LB, dk=dk, dv=dv, scale=dk ** -0.5, out_scale=1.0 - lam_init),
        out_shape=jax.ShapeDtypeStruct((R, H * dv), BF16),
        grid=(B, H, nQ),
        in_specs=[pl.BlockSpec(memory_space=pltpu.SMEM),
                  pl.BlockSpec((T, dv), lambda b, h, i: (b * nQ + i, qc + h)),
                  pl.BlockSpec((LB, dv), lambda b, h, i: (b, kc + h)),
                  pl.BlockSpec((LB, dv), lambda b, h, i: (b, vc + h)),
                  pl.BlockSpec((6, 1, T, T), lambda b, h, i: (0, h, 0, 0)),
                  pl.BlockSpec((1, dv), lambda b, h, i: (0, 0))],
        out_specs=pl.BlockSpec((T, dv), lambda b, h, i: (b * nQ + i, h)),
        scratch_shapes=[pltpu.VMEM((2 * T, dv), BF16), pltpu.VMEM((LB, dv), BF16), pltpu.VMEM((nQ, dv, T), BF16),
                        pltpu.VMEM((8, 2 * T), F32), pltpu.VMEM((8, 2 * T), F32),
                        pltpu.VMEM((dv // 8, 8, 2 * T), F32)],
        compiler_params=_cparams(("parallel", "parallel", "arbitrary")),
        name="attn",
    )(lam, proj, proj, proj, jnp.swapaxes(bank, -1, -2), subln.reshape(1, dv))


def _log_forget(zf, log_lb, log1m_lb):
    ls = jnp.minimum(zf, 0.0) - jnp.log1p(jnp.exp(-jnp.abs(zf)))
    b = log1m_lb + ls
    hi = jnp.maximum(log_lb, b)
    lo = jnp.minimum(log_lb, b)
    return hi + jnp.log1p(jnp.exp(lo - hi))


def _split3(x):
    hi = x.astype(BF16)
    r = x - hi.astype(F32)
    mid = r.astype(BF16)
    lo = (r - mid.astype(F32)).astype(BF16)
    return hi, mid, lo


def _hgrn_kernel(q_ref, f_ref, i_ref, g_ref, loglb_ref, log1m_ref, omlb_ref, gn_ref,
                 o_ref, s_ref, st_ref, *, T, n_pad, H, dk):
    t = pl.program_id(1)

    @pl.when(t == 0)
    def _():
        st_ref[...] = jnp.zeros(st_ref.shape, F32)

    U = 8
    rr = lax.broadcasted_iota(jnp.int32, (SUB, SUB), 0)
    cc = lax.broadcasted_iota(jnp.int32, (SUB, SUB), 1)
    tri = jnp.where(cc <= rr, 1.0, 0.0).astype(BF16)
    row = lax.broadcasted_iota(jnp.int32, (SUB, 1), 0)
    urow = lax.broadcasted_iota(jnp.int32, (U, 1), 0)

    def chunk(c, carry):
        r0 = pl.multiple_of(c * SUB, SUB)
        zf = f_ref[pl.ds(r0, SUB), :]
        logf = _log_forget(zf, loglb_ref[...], log1m_ref[...])
        hi, mid, lo = _split3(logf)
        bcum = (jnp.dot(tri, hi, preferred_element_type=F32) + jnp.dot(tri, mid, preferred_element_type=F32)
                + jnp.dot(tri, lo, preferred_element_type=F32)) * LOG2E
        kd = omlb_ref[...] * _sigmoid(-zf)
        zq = q_ref[pl.ds(r0, SUB), :]
        qd = zq * _sigmoid(zq)
        seq_row = t * T + r0 + row
        vd = jnp.where(seq_row >= n_pad, i_ref[pl.ds(r0, SUB), :], 0.0)
        zg = g_ref[pl.ds(r0, SUB), :]
        og = zg * _sigmoid(zg)
        blast, bmid = bcum[SUB - 1:SUB, :], bcum[U - 1:U, :]
        qs = (qd * jnp.exp2(bcum)).astype(BF16)
        kt = (kd * jnp.exp2(blast - bcum)).astype(BF16)
        q1 = (qd[U:SUB] * jnp.exp2(bcum[U:SUB] - bmid)).astype(BF16)
        k0 = (kd[0:U] * jnp.exp2(bmid - bcum[0:U])).astype(BF16)
        v16 = vd.astype(BF16)
        glast = jnp.exp2(blast)
        o_state, a10 = [], []
        for h in range(H):
            sl = slice(h * dk, (h + 1) * dk)
            st = st_ref[h]
            o_state.append(lax.dot_general(qs[:, sl], st.astype(BF16), (((1,), (1,)), ((), ())),
                                           preferred_element_type=F32))
            a10.append(lax.dot_general(q1[:, sl], k0[:, sl], (((1,), (1,)), ((), ())),
                                       preferred_element_type=F32))
            kv = lax.dot_general(v16[:, sl], kt[:, sl], (((0,), (0,)), ((), ())), preferred_element_type=F32)
            st_ref[h] = st * glast[:, sl] + kv
        intra = []
        for h in range(H):
            sl = slice(h * dk, (h + 1) * dk)
            units = []
            for u in range(SUB // U):
                us = slice(u * U, (u + 1) * U)
                bu, qu, ku, vu = bcum[us, sl], qd[us, sl], kd[us, sl], vd[us, sl]
                a = None
                for d in range(U):
                    if d == 0:
                        kr, br, vr = ku, bu, vu
                    else:
                        kr, br, vr = pltpu.roll(ku, d, 0), pltpu.roll(bu, d, 0), pltpu.roll(vu, d, 0)
                    w = jnp.where(urow >= d, qu * kr * jnp.exp2(bu - br), 0.0)
                    term = jnp.sum(w, axis=-1, keepdims=True) * vr
                    a = term if a is None else a + term
                units.append(a)
            intra.append(jnp.concatenate(units, axis=0))
        for h in range(H):
            sl = slice(h * dk, (h + 1) * dk)
            cross = jnp.dot(a10[h].astype(BF16), v16[0:U, sl], preferred_element_type=F32)
            o = o_state[h] + intra[h] + jnp.concatenate([jnp.zeros((U, dk), F32), cross], axis=0)
            o_ref[pl.ds(r0, SUB), sl] = (_rms(o, gn_ref[...]) * og[:, sl]).astype(o_ref.dtype)
        return carry

    lax.fori_loop(0, T // SUB, chunk, 0)

    @pl.when(t == pl.num_programs(1) - 1)
    def _():
        for h in range(H):
            s_ref[0, h] = st_ref[h].T


def _hgrn(proj, log_lb, log1m_lb, om_lb, g_norm, *, B, LB, n_pad, H, dk):
    T = SEQ_TILE
    nT = LB // T
    R = proj.shape[0]
    W = H * dk

    def seg(c):
        return pl.BlockSpec((T, W), lambda b, t: (b * nT + t, c))

    def par(n):
        return pl.BlockSpec((1, n), lambda b, t: (0, 0))

    return pl.pallas_call(
        functools.partial(_hgrn_kernel, T=T, n_pad=n_pad, H=H, dk=dk),
        out_shape=(jax.ShapeDtypeStruct((R, W), BF16), jax.ShapeDtypeStruct((B, H, dk, dk), F32)),
        grid=(B, nT),
        in_specs=[seg(8), seg(9), seg(10), seg(11), par(W), par(W), par(W), par(dk)],
        out_specs=(pl.BlockSpec((T, W), lambda b, t: (b * nT + t, 0)),
                   pl.BlockSpec((1, H, dk, dk), lambda b, t: (b, 0, 0, 0))),
        scratch_shapes=[pltpu.VMEM((H, dk, dk), F32)],
        compiler_params=_cparams(("parallel", "arbitrary")),
        name="hgrn",
    )(proj, proj, proj, proj, log_lb, log1m_lb, om_lb, g_norm.reshape(1, dk))


def _sample_mix_kernel(p_ref, sa_ref, sb_ref, sh_ref, dwa_ref, dwab_ref, lng_ref, lnb_ref, cb_ref,
                       lb_ref, omlb_ref, gn_ref, pa_in, pb_in, pd_in,
                       pa_ref, pb_ref, pd_ref, nsa_ref, nsb_ref, nsh_ref, o_scr, *, NB, W, H, dk, wa, wb):
    del pa_in, pb_in, pd_in

    def seg(c):
        return p_ref[:, c * W:(c + 1) * W]

    pad = jnp.zeros((SUB - NB, W), F32)

    glu = seg(0) * _sigmoid(seg(1))
    rows = []
    for b in range(NB):
        cv = jnp.sum(sa_ref[b] * dwa_ref[0:wa - 1, :], axis=0, keepdims=True)
        rows.append(cv + dwa_ref[wa - 1:wa, :] * glu[b:b + 1, :])
        nsa_ref[b, 0:wa - 2, :] = sa_ref[b, 1:wa - 1, :]
        nsa_ref[b, wa - 2:wa - 1, :] = glu[b:b + 1, :]
    acc = jnp.concatenate(rows, axis=0) + dwab_ref[...]
    mu = jnp.mean(acc, axis=-1, keepdims=True)
    cen = acc - mu
    var = jnp.mean(cen * cen, axis=-1, keepdims=True)
    y = cen * lax.rsqrt(var + EPS) * lng_ref[...] + lnb_ref[...]
    pa_ref[...] = jnp.concatenate([y * _sigmoid(y), pad], axis=0).astype(pa_ref.dtype)

    u = seg(3) * seg(4)
    rows = []
    for b in range(NB):
        cv = jnp.sum(sb_ref[b] * cb_ref[0:wb - 1, :], axis=0, keepdims=True)
        rows.append(cv + cb_ref[wb - 1:wb, :] * u[b:b + 1, :])
        if wb > 2:
            nsb_ref[b, 0:wb - 2, :] = sb_ref[b, 1:wb - 1, :]
        nsb_ref[b, wb - 2:wb - 1, :] = u[b:b + 1, :]
    pb_ref[...] = jnp.concatenate([seg(2) * jnp.concatenate(rows, axis=0), pad], axis=0).astype(pb_ref.dtype)

    zf = seg(9)
    sg = _sigmoid(zf)
    fg = lb_ref[...] + omlb_ref[...] * sg
    kd = omlb_ref[...] * _sigmoid(-zf)
    zq = seg(8)
    qd = zq * _sigmoid(zq)
    vd = seg(10)
    eye = lax.broadcasted_iota(jnp.int32, (dk, dk), 0) == lax.broadcasted_iota(jnp.int32, (dk, dk), 1)

    def col(x):
        return jnp.sum(jnp.where(eye, x, 0.0), axis=1, keepdims=True)

    for b in range(NB):
        for h in range(H):
            sl = slice(h * dk, (h + 1) * dk)
            s_new = col(fg[b:b + 1, sl]) * sh_ref[b, h] + col(kd[b:b + 1, sl]) * vd[b:b + 1, sl]
            nsh_ref[b, h] = s_new
            o_scr[b:b + 1, sl] = jnp.sum(col(qd[b:b + 1, sl]) * s_new, axis=0, keepdims=True)
    zg = seg(11)
    og = zg * _sigmoid(zg)
    o = o_scr[...]
    outs = [_rms(o[:, h * dk:(h + 1) * dk], gn_ref[...]) for h in range(H)]
    pd_ref[...] = jnp.concatenate([jnp.concatenate(outs, axis=1) * og, pad], axis=0).astype(pd_ref.dtype)


def _sample_mix(proj, sa, sb, sh, dw_a, dw_a_bias, ln_g, ln_b, conv_b, lb, om_lb, g_norm, pre_a, pre_b, pre_d):
    NB, wa1, W = sa.shape
    wb1 = sb.shape[1]
    H, dk = sh.shape[1], sh.shape[2]
    R = proj.shape[0]

    def full(shape):
        return pl.BlockSpec(shape, lambda i: (0,) * len(shape))

    rows = pl.BlockSpec((SUB, W), lambda i: (0, 0))
    return pl.pallas_call(
        functools.partial(_sample_mix_kernel, NB=NB, W=W, H=H, dk=dk, wa=wa1 + 1, wb=wb1 + 1),
        out_shape=(jax.ShapeDtypeStruct((R, W), BF16), jax.ShapeDtypeStruct((R, W), BF16),
                   jax.ShapeDtypeStruct((R, W), BF16), jax.ShapeDtypeStruct(sa.shape, F32),
                   jax.ShapeDtypeStruct(sb.shape, F32), jax.ShapeDtypeStruct(sh.shape, F32)),
        grid=(1,),
        in_specs=[pl.BlockSpec((NB, N_SEG * W), lambda i: (0, 0)), full(sa.shape), full(sb.shape), full(sh.shape),
                  full((wa1 + 1, W)), full((1, W)), full((1, W)), full((1, W)), full((wb1 + 1, W)),
                  full((1, W)), full((1, W)), full((1, dk)),
                  pl.BlockSpec(memory_space=pl.ANY), pl.BlockSpec(memory_space=pl.ANY),
                  pl.BlockSpec(memory_space=pl.ANY)],
        out_specs=(rows, rows, rows, full(sa.shape), full(sb.shape), full(sh.shape)),
        scratch_shapes=[pltpu.VMEM((NB, W), F32)],
        input_output_aliases={12: 0, 13: 1, 14: 2},
        compiler_params=_cparams(("arbitrary",)),
        name="sample_mix",
    )(proj, sa, sb, sh, dw_a, dw_a_bias.reshape(1, W), ln_g.reshape(1, W), ln_b.reshape(1, W), conv_b,
      lb, om_lb, g_norm.reshape(1, dk), pre_a, pre_b, pre_d)


def _decode_kernel(pt_ref, lam_ref, q_ref, kn_ref, vn_ref, bias_ref, biasn_ref, sub_ref, pc_in, *rest,
                   G, P, H, dv, scale, out_scale, NB):
    del pt_ref, pc_in
    k_refs, v_refs = rest[:G], rest[G:2 * G]
    o_ref, m_ref, l_ref, acc_ref, o_scr = rest[2 * G:]
    b = pl.program_id(0)
    s = pl.program_id(1)
    dk = dv // 2
    HM = 2 * H
    W = H * dv
    C = P * H
    last = s == pl.num_programs(1) - 1

    @pl.when(s == 0)
    def _():
        m_ref[...] = jnp.full(m_ref.shape, NEG, F32)
        l_ref[...] = jnp.zeros(l_ref.shape, F32)
        acc_ref[...] = jnp.zeros(acc_ref.shape, F32)

    def per_map_rows(row):
        return jnp.concatenate([row[:, h * dv:(h + 1) * dv] for h in range(H) for _ in range(2)], axis=0)

    rr = lax.broadcasted_iota(jnp.int32, (HM, dv), 0)
    ll = lax.broadcasted_iota(jnp.int32, (HM, dv), 1)
    own_map = (ll >= dk) == (jnp.bitwise_and(rr, 1) == 1)
    qall = jnp.where(own_map, per_map_rows(q_ref[pl.ds(b, 1), :]) * (scale * LOG2E), 0.0)
    qall16 = qall.astype(BF16)
    col_head = jnp.bitwise_and(lax.broadcasted_iota(jnp.int32, (HM, C), 1), H - 1)
    row_head = lax.shift_right_logical(lax.broadcasted_iota(jnp.int32, (HM, C), 0), 1)
    own_head = col_head == row_head

    blocks = []
    for g in range(G):
        kg = k_refs[g][...].reshape(C, dv).astype(BF16)
        sg = lax.dot_general(qall16, kg, (((1,), (1,)), ((), ())), preferred_element_type=F32)
        if g == G - 1:
            sg = sg + jnp.where(last, bias_ref[...], 0.0)
        blocks.append(jnp.where(own_head, sg, NEG))
    sc = jnp.concatenate(blocks, axis=1)
    m_old = m_ref[...]
    m_new = jnp.maximum(m_old, jnp.max(sc, axis=-1, keepdims=True))
    alpha = jnp.exp2(m_old - m_new)
    p = jnp.exp2(sc - m_new)
    l_ref[...] = alpha * l_ref[...] + jnp.sum(p, axis=-1, keepdims=True)
    pv = None
    for g in range(G):
        vg = v_refs[g][...].reshape(C, dv).astype(BF16)
        y = jnp.dot(p[:, g * C:(g + 1) * C].astype(BF16), vg, preferred_element_type=F32)
        pv = y if pv is None else pv + y
    acc_ref[...] = alpha * acc_ref[...] + pv
    m_ref[...] = m_new

    @pl.when(last)
    def _():
        kn = per_map_rows(kn_ref[pl.ds(b, 1), :])
        vn = per_map_rows(vn_ref[pl.ds(b, 1), :])
        sn = jnp.sum(qall * kn, axis=-1, keepdims=True) + biasn_ref[...]
        m1 = m_ref[...]
        m2 = jnp.maximum(m1, sn)
        a2 = jnp.exp2(m1 - m2)
        pn = jnp.exp2(sn - m2)
        o = (a2 * acc_ref[...] + pn * vn) / (a2 * l_ref[...] + pn)
        lam = lam_ref[0]
        outs = []
        for h in range(H):
            oh = o[2 * h:2 * h + 1, :] - lam * o[2 * h + 1:2 * h + 2, :]
            outs.append(_rms(oh, sub_ref[...]) * out_scale)
        o_scr[pl.ds(b, 1), :] = jnp.concatenate(outs, axis=1)

    @pl.when((s == pl.num_programs(1) - 1) & (b == NB - 1))
    def _():
        o_ref[...] = jnp.concatenate([o_scr[...], jnp.zeros((SUB - NB, W), F32)], axis=0).astype(o_ref.dtype)


def _decode_attn(proj, cache_k, cache_v, page_table, layer, bias_past, bias_new, lam, subln, pre_c,
                 *, H, dv, lam_init):
    NB, n_pages = page_table.shape
    P = cache_k.shape[2]
    G = PAGES_PER_STEP
    assert n_pages % G == 0 and P >= MAX_DISTANCE and H & (H - 1) == 0
    W = H * dv
    HM = 2 * H

    def page(g):
        return pl.BlockSpec((None, None, P, H, dv), lambda b, s, pt: (pt[b, s * G + g], layer, 0, 0, 0))

    def rows(c):
        return pl.BlockSpec((NB, W), lambda b, s, pt: (0, c))

    grid_spec = pltpu.PrefetchScalarGridSpec(
        num_scalar_prefetch=1,
        grid=(NB, n_pages // G),
        in_specs=[pl.BlockSpec(memory_space=pltpu.SMEM), rows(5), rows(6), rows(7),
                  pl.BlockSpec((HM, P * H), lambda b, s, pt: (0, 0)),
                  pl.BlockSpec((HM, 1), lambda b, s, pt: (0, 0)),
                  pl.BlockSpec((1, dv), lambda b, s, pt: (0, 0)),
                  pl.BlockSpec(memory_space=pl.ANY)]
        + [page(g) for g in range(G)] + [page(g) for g in range(G)],
        out_specs=pl.BlockSpec((SUB, W), lambda b, s, pt: (0, 0)),
        scratch_shapes=[pltpu.VMEM((HM, 1), F32), pltpu.VMEM((HM, 1), F32), pltpu.VMEM((HM, dv), F32),
                        pltpu.VMEM((NB, W), F32)],
    )
    return pl.pallas_call(
        functools.partial(_decode_kernel, G=G, P=P, H=H, dv=dv, scale=(dv // 2) ** -0.5,
                          out_scale=1.0 - lam_init, NB=NB),
        out_shape=jax.ShapeDtypeStruct(pre_c.shape, pre_c.dtype),
        grid_spec=grid_spec,
        input_output_aliases={8: 0},
        compiler_params=_cparams(("arbitrary", "arbitrary")),
        name="decode_attn",
    )(page_table, lam, proj, proj, proj, bias_past, bias_new, subln.reshape(1, dv), pre_c,
      *([cache_k] * G), *([cache_v] * G))


def kernel(x_prompt, x_sample, cache_k, cache_v, state_conv_a, state_conv_b, state_hgrn, page_table, meta_tokens, rel_bias_table, hgrn_lower_bound, norm1, w_in, dw_a, dw_a_bias, ln_a_g, ln_a_b, w_a_out, conv_b, w_b_out, lam_q1, lam_k1, lam_q2, lam_k2, subln, w_c_out, g_norm_d, w_d_out, w_gate, b_gate, w_o, norm2, w_up, w_down, final_norm):
    B, L, D = x_prompt.shape
    NB = x_sample.shape[0]
    assert x_sample.shape[1] == 1
    depth = w_in.shape[0]
    n_meta = meta_tokens.shape[0]
    W = state_conv_a.shape[-1]
    H, dv = cache_v.shape[3], cache_v.shape[4]
    Hr, dkr = state_hgrn.shape[2], state_hgrn.shape[3]
    n_buckets = rel_bias_table.shape[0]
    past_len = page_table.shape[1] * cache_k.shape[2]
    assert w_in.shape[2] == N_SEG * W and H * dv == W and Hr * dkr == W and state_conv_b.shape[-1] == W
    assert cache_k.shape[4] == dv and state_hgrn.shape[4] == dkr and dv == LANES and dkr == LANES

    T = SEQ_TILE
    Lt = n_meta + L
    LB = -(-(Lt + NB) // T) * T
    n_pad = LB - Lt
    R = B * LB
    tm = ROW_TILE if R % ROW_TILE == 0 else T
    assert n_pad <= T and NB <= SUB <= n_pad and R % tm == 0

    meta = jnp.broadcast_to(meta_tokens[None].astype(F32), (B, n_meta, D))
    h = jnp.concatenate([jnp.zeros((B, n_pad, D), F32), meta, x_prompt], axis=1)
    h = h.at[0, :NB].set(x_sample[:, 0]).reshape(R, D)

    lbs = jnp.cumsum(jax.nn.softmax(hgrn_lower_bound.astype(F32), axis=0), axis=0)
    lbs = lbs - lbs[0:1]
    bank = _bias_bank(rel_bias_table, T, n_pad)
    pos_s = jnp.full((1,), past_len, jnp.int32)
    P = cache_k.shape[2]
    kpos_near = jnp.arange(past_len - P, past_len + 1)
    bias_dec = jnp.repeat(_rel_bias(pos_s, kpos_near, rel_bias_table, n_buckets)[:, 0, :], 2, axis=0) * LOG2E
    bias_past, bias_new = jnp.repeat(bias_dec[:, :P], H, axis=1), bias_dec[:, P:]

    w_in16, w_gate16, w_o16 = w_in.astype(BF16), w_gate.astype(BF16), w_o.astype(BF16)
    w_up16, w_down16 = w_up.astype(BF16), w_down.astype(BF16)
    w_branch16 = jnp.stack([w_a_out, w_b_out, w_c_out, w_d_out], axis=1).astype(BF16)

    k_p, v_p, ca_p, cb_p, s_p = [], [], [], [], []
    k_s, v_s, ca_s, cb_s, s_s = [], [], [], [], []
    for l in range(depth):
        lam_init = 0.8 - 0.6 * math.exp(-0.3 * l)
        lam = (jnp.exp(jnp.sum(lam_q1[l].astype(F32) * lam_k1[l].astype(F32)))
               - jnp.exp(jnp.sum(lam_q2[l].astype(F32) * lam_k2[l].astype(F32))) + lam_init).reshape(1)
        lb = lbs[l].reshape(1, W)
        log_lb, log1m_lb, om_lb = jnp.log(lb), jnp.log1p(-lb), 1.0 - lb

        proj, kv = _matmul(h, w_in16, l, tm=tm, tn=COL_TILE, out_dtype=F32, norm_g=norm1[l],
                           heads=(6, 2, H, dv), name="proj")
        gates = _matmul(h, w_gate16, l, tm=tm, tn=COL_TILE, out_dtype=BF16, norm_g=norm1[l],
                        bias=b_gate[l], act="sigmoid", name="gates")

        pre_a, pre_b, ca, cb = _conv(proj, dw_a[l], dw_a_bias[l], ln_a_g[l], ln_a_b[l], conv_b[l],
                                     B=B, LB=LB, n_pad=n_pad, W=W)
        pre_c = _attn(proj, bank, lam, subln[l], B=B, LB=LB, H=H, dv=dv, lam_init=lam_init)
        pre_d, s_fin = _hgrn(proj, log_lb, log1m_lb, om_lb, g_norm_d[l], B=B, LB=LB, n_pad=n_pad, H=Hr, dk=dkr)

        pre_a, pre_b, pre_d, nsa, nsb, nsh = _sample_mix(
            proj, state_conv_a[l], state_conv_b[l], state_hgrn[l], dw_a[l], dw_a_bias[l], ln_a_g[l], ln_a_b[l],
            conv_b[l], lb, om_lb, g_norm_d[l], pre_a, pre_b, pre_d)
        pre_c = _decode_attn(proj, cache_k, cache_v, page_table, l, bias_past, bias_new, lam, subln[l], pre_c,
                             H=H, dv=dv, lam_init=lam_init)

        merged = _merge((pre_a, pre_b, pre_c, pre_d), w_branch16, l, gates, tm=tm, tn=MERGE_COL_TILE)
        h = _matmul(merged, w_o16, l, tm=tm, tn=COL_TILE, out_dtype=F32, residual=h, name="w_o")
        last = l == depth - 1
        h = _mlp(h, norm2[l], w_up16, w_down16, l, final_norm if last else None, tm=tm, tf=FF_TILE)

        kv5 = kv.reshape(2, B, LB, H, dv)
        k_p.append(kv5[0, :, n_pad:])
        v_p.append(kv5[1, :, n_pad:])
        k_s.append(kv[0, :NB])
        v_s.append(kv[1, :NB])
        ca_p.append(ca); cb_p.append(cb); s_p.append(s_fin)
        ca_s.append(nsa); cb_s.append(nsb); s_s.append(nsh)

    y3 = h.reshape(B, LB, D)
    y_prompt = y3[:, n_pad + n_meta:]
    y_sample = y3[0, :NB].reshape(NB, 1, D)
    new_k_prompt = jnp.stack(k_p, axis=2)
    new_v_prompt = jnp.stack(v_p, axis=2)
    new_k_sample = jnp.stack(k_s, axis=1).reshape(NB, 1, depth, H, dv)
    new_v_sample = jnp.stack(v_s, axis=1).reshape(NB, 1, depth, H, dv)
    return (y_prompt, y_sample, new_k_prompt, new_v_prompt, jnp.stack(ca_p, 0), jnp.stack(cb_p, 0),
            jnp.stack(s_p, 0), new_k_sample, new_v_sample, jnp.stack(ca_s, 0), jnp.stack(cb_s, 0),
            jnp.stack(s_s, 0))
```
